```python
import math
import jax, jax.numpy as jnp
from jax import lax
import numpy as np

D_MODEL = 2048
BATCH = 4
SEQ = 8192
DEPTH = 1
DEC_BATCH = 32
DEC_SEQ = 64
PAST_LEN = 4096

CHUNK = 64
QBLOCK = 128
DA_HEADS = 8
DA_HD = 64
DA_WIDTH = DA_HEADS * 2 * DA_HD
RW_HEADS = 16
RW_HD = 64
RW_WIDTH = RW_HEADS * RW_HD
RW_DECAY_LORA = 96
RW_AAA_LORA = 96
RW_GATE_LORA = 256
RW_PROJ = 3 * RW_WIDTH + RW_DECAY_LORA + RW_AAA_LORA + RW_GATE_LORA
FFN_DIM = 11 * D_MODEL // 4
IN_COLS = 3 * DA_WIDTH + RW_PROJ + 2 * D_MODEL
SPLIT_IN = (DA_WIDTH, 2 * DA_WIDTH, 3 * DA_WIDTH, 3 * DA_WIDTH + RW_PROJ, 3 * DA_WIDTH + RW_PROJ + D_MODEL)
RW_SPLIT = (RW_WIDTH, 2 * RW_WIDTH, 3 * RW_WIDTH, 3 * RW_WIDTH + RW_DECAY_LORA,
            3 * RW_WIDTH + RW_DECAY_LORA + RW_AAA_LORA)
NORM_EPS = 1e-6
SUBLN_EPS = 1e-5
LNX_EPS = 64e-5

kernel_name = "hybrid_diffattn_rwkv7_macaron_step"


def rmsnorm(x, g, eps=NORM_EPS):
    xf = x.astype(jnp.float32)
    y = xf * lax.rsqrt(jnp.mean(xf * xf, axis=-1, keepdims=True) + eps)
    return (y * g.astype(jnp.float32)).astype(x.dtype)


def swiglu(h, wi, wo):
    gate, up = jnp.split(h @ wi, 2, axis=-1)
    return (jax.nn.silu(gate) * up) @ wo


def diff_attend(q, k, v, q_pos, k_pos, lam):
    s = jnp.einsum('bqhcd,bkhcd->bhcqk', q, k).astype(jnp.float32) * (DA_HD ** -0.5)
    mask = (k_pos[None, :] // CHUNK) <= (q_pos[:, None] // CHUNK)
    s = jnp.where(mask[None, None, None], s, -jnp.inf)
    p = jax.nn.softmax(s, axis=-1)
    attn = p[:, :, 0] - lam * p[:, :, 1]
    return jnp.einsum('bhqk,bkhe->bqhe', attn.astype(v.dtype), v)


def diff_attn_prompt(q, k, v, lam):
    B, S = q.shape[0], q.shape[1]
    nblk = S // QBLOCK
    qb = jnp.moveaxis(q.reshape(B, nblk, QBLOCK, DA_HEADS, 2, DA_HD), 1, 0)
    starts = jnp.arange(nblk, dtype=jnp.int32) * QBLOCK
    k_pos = jnp.arange(S, dtype=jnp.int32)

    def one_block(args):
        qi, st = args
        return diff_attend(qi, k, v, st + jnp.arange(QBLOCK, dtype=jnp.int32), k_pos, lam)

    o = lax.map(one_block, (qb, starts))
    return jnp.moveaxis(o, 0, 1).reshape(B, S, DA_HEADS, 2 * DA_HD)


def diff_post(o, g, lam_init):
    of = o.astype(jnp.float32)
    of = of * lax.rsqrt(jnp.mean(of * of, axis=-1, keepdims=True) + SUBLN_EPS)
    of = of * g.astype(jnp.float32) * (1.0 - lam_init)
    return of.astype(o.dtype).reshape(o.shape[0], o.shape[1], DA_WIDTH)


def rwkv7_mix(u, prev, s0, mu, w0, w2, a0, a2, g2, k_k, k_a, r_k, lnx_g, lnx_b):
    B, T = u.shape[0], u.shape[1]
    f32 = jnp.float32
    u_prev = jnp.concatenate([prev.astype(u.dtype), u[:, :-1]], axis=1)
    us = u + mu * (u_prev - u)
    r, k, v, wl, al, gl = jnp.split(us, RW_SPLIT, axis=-1)
    w = -jax.nn.softplus(-(w0 + jnp.tanh(wl) @ w2).astype(f32)) - 0.5
    decay = jnp.exp(-jnp.exp(w))
    a = jax.nn.sigmoid((a0 + al @ a2).astype(f32))
    g = jax.nn.sigmoid(gl) @ g2

    def hd(t):
        return t.astype(f32).reshape(B, T, RW_HEADS, RW_HD)

    r_h, k_h, v_h, a_h, w_h = hd(r), hd(k), hd(v), hd(a), hd(decay)
    kk = k_h * k_k.astype(f32).reshape(RW_HEADS, RW_HD)
    kk = kk * lax.rsqrt(jnp.maximum(jnp.sum(kk * kk, axis=-1, keepdims=True), 1e-24))
    k_h = k_h * (1.0 + (a_h - 1.0) * k_a.astype(f32).reshape(RW_HEADS, RW_HD))

    def tm(t):
        return jnp.moveaxis(t, 1, 0)

    def step(S, xs):
        r_t, w_t, k_t, v_t, kk_t, a_t = xs
        sa = jnp.einsum('bhij,bhj->bhi', S, -kk_t)
        S = (S * w_t[:, :, None, :] + sa[..., None] * (kk_t * a_t)[:, :, None, :]
             + v_t[..., None] * k_t[:, :, None, :])
        return S, jnp.einsum('bhij,bhj->bhi', S, r_t)

    s_fin, y = lax.scan(step, s0.astype(f32), (tm(r_h), tm(w_h), tm(k_h), tm(v_h), tm(kk), tm(a_h)))
    y = jnp.moveaxis(y, 0, 1)
    mean = jnp.mean(y, axis=-1, keepdims=True)
    var = jnp.mean(jnp.square(y - mean), axis=-1, keepdims=True)
    y = (y - mean) * lax.rsqrt(var + LNX_EPS)
    y = y * lnx_g.astype(f32).reshape(RW_HEADS, RW_HD) + lnx_b.astype(f32).reshape(RW_HEADS, RW_HD)
    y = y + jnp.sum(r_h * k_h * r_k.astype(f32), axis=-1, keepdims=True) * v_h
    o = y.reshape(B, T, RW_WIDTH).astype(u.dtype) * g
    return o, s_fin.astype(s0.dtype)


def layer(x, l, past_k, past_v, rw_state, shift_prev, p):
    B, T = x.shape[0], x.shape[1]
    x = x + 0.5 * swiglu(rmsnorm(x, p['ffn1_norm']), p['ffn1_wi'], p['ffn1_wo'])
    h = rmsnorm(x, p['mix_norm'])
    u = h @ p['w_in']
    q, k, v, urw, ga, gb = jnp.split(u, SPLIT_IN, axis=-1)
    q5 = q.reshape(B, T, DA_HEADS, 2, DA_HD)
    k4 = k.reshape(B, T, DA_HEADS, 2 * DA_HD)
    v4 = v.reshape(B, T, DA_HEADS, 2 * DA_HD)
    lam_init = 0.8 - 0.6 * math.exp(-0.3 * l)
    f32 = jnp.float32
    lam = (jnp.exp(jnp.sum(p['lambda_q1'].astype(f32) * p['lambda_k1'].astype(f32)))
           - jnp.exp(jnp.sum(p['lambda_q2'].astype(f32) * p['lambda_k2'].astype(f32))) + lam_init)
    if past_k is None:
        o_a = diff_attn_prompt(q5, k4.reshape(B, T, DA_HEADS, 2, DA_HD), v4, lam)
        s0 = jnp.zeros((B, RW_HEADS, RW_HD, RW_HD), jnp.float32)
        prev = jnp.zeros((B, 1, RW_PROJ), u.dtype)
    else:
        P = past_k.shape[1]
        k_all = jnp.concatenate([past_k.astype(k4.dtype), k4], axis=1)
        v_all = jnp.concatenate([past_v.astype(v4.dtype), v4], axis=1)
        o_a = diff_attend(q5, k_all.reshape(B, P + T, DA_HEADS, 2, DA_HD), v_all,
                          P + jnp.arange(T, dtype=jnp.int32), jnp.arange(P + T, dtype=jnp.int32), lam)
        s0 = rw_state
        prev = shift_prev
    o_a = diff_post(o_a, p['subln_g'], lam_init)
    o_b, s_new = rwkv7_mix(urw, prev, s0, p['shift_mu'], p['rw_w0'], p['rw_w2'], p['rw_a0'], p['rw_a2'],
                           p['rw_g2'], p['rw_k_k'], p['rw_k_a'], p['rw_r_k'], p['rw_lnx_g'], p['rw_lnx_b'])
    merged = jax.nn.sigmoid(ga) * (o_a @ p['w_proj_a']) + jax.nn.sigmoid(gb) * (o_b @ p['w_proj_b'])
    x = x + merged @ p['w_out']
    x = x + 0.5 * swiglu(rmsnorm(x, p['ffn2_norm']), p['ffn2_wi'], p['ffn2_wo'])
    return x, k4, v4, s_new, urw[:, -1:]


def setup_inputs(seed: int = 0) -> dict:
    key = jax.random.key(seed)
    ks = jax.random.split(key, 40)
    f32 = jnp.float32
    L = DEPTH

    def nrm(k, shape, scale):
        return jax.random.normal(k, shape, f32) * scale

    inp = {}
    inp['x_prompt'] = nrm(ks[0], (BATCH, SEQ, D_MODEL), 1.0)
    inp['x_sample'] = nrm(ks[1], (DEC_BATCH, DEC_SEQ, D_MODEL), 1.0)
    inp['cache_k'] = nrm(ks[2], (L, DEC_BATCH, PAST_LEN, DA_HEADS, 2 * DA_HD), 1.0)
    inp['cache_v'] = nrm(ks[3], (L, DEC_BATCH, PAST_LEN, DA_HEADS, 2 * DA_HD), 1.0)
    inp['state_rwkv'] = nrm(ks[4], (L, DEC_BATCH, RW_HEADS, RW_HD, RW_HD), 1.0)
    inp['state_shift'] = nrm(ks[5], (L, DEC_BATCH, 1, RW_PROJ), 1.0)
    inp['ffn1_norm'] = 1.0 + nrm(ks[6], (L, D_MODEL), 0.02)
    inp['ffn1_wi'] = nrm(ks[7], (L, D_MODEL, 2 * FFN_DIM), D_MODEL ** -0.5)
    inp['ffn1_wo'] = nrm(ks[8], (L, FFN_DIM, D_MODEL), FFN_DIM ** -0.5)
    inp['mix_norm'] = 1.0 + nrm(ks[9], (L, D_MODEL), 0.02)
    inp['w_in'] = nrm(ks[10], (L, D_MODEL, IN_COLS), D_MODEL ** -0.5)
    inp['lambda_q1'] = nrm(ks[11], (L, DA_HD), 0.1)
    inp['lambda_k1'] = nrm(ks[12], (L, DA_HD), 0.1)
    inp['lambda_q2'] = nrm(ks[13], (L, DA_HD), 0.1)
    inp['lambda_k2'] = nrm(ks[14], (L, DA_HD), 0.1)
    inp['subln_g'] = 1.0 + nrm(ks[15], (L, 2 * DA_HD), 0.02)
    inp['shift_mu'] = jax.random.uniform(ks[16], (L, RW_PROJ), f32)
    inp['rw_w0'] = jnp.linspace(-6.0, -1.0, RW_WIDTH, dtype=f32)[None, :] + nrm(ks[17], (L, RW_WIDTH), 0.1)
    inp['rw_w2'] = nrm(ks[18], (L, RW_DECAY_LORA, RW_WIDTH), 0.5 * RW_DECAY_LORA ** -0.5)
    inp['rw_a0'] = nrm(ks[19], (L, RW_WIDTH), 0.1)
    inp['rw_a2'] = nrm(ks[20], (L, RW_AAA_LORA, RW_WIDTH), 0.5 * RW_AAA_LORA ** -0.5)
    inp['rw_g2'] = nrm(ks[21], (L, RW_GATE_LORA, RW_WIDTH), RW_GATE_LORA ** -0.5)
    inp['rw_k_k'] = 0.85 + nrm(ks[22], (L, RW_WIDTH), 0.05)
    inp['rw_k_a'] = 1.0 + nrm(ks[23], (L, RW_WIDTH), 0.05)
    inp['rw_r_k'] = nrm(ks[24], (L, RW_HEADS, RW_HD), 0.1)
    inp['rw_lnx_g'] = 1.0 + nrm(ks[25], (L, RW_WIDTH), 0.02)
    inp['rw_lnx_b'] = nrm(ks[26], (L, RW_WIDTH), 0.01)
    inp['w_proj_a'] = nrm(ks[27], (L, DA_WIDTH, D_MODEL), DA_WIDTH ** -0.5)
    inp['w_proj_b'] = nrm(ks[28], (L, RW_WIDTH, D_MODEL), RW_WIDTH ** -0.5)
    inp['w_out'] = nrm(ks[29], (L, D_MODEL, D_MODEL), D_MODEL ** -0.5)
    inp['ffn2_norm'] = 1.0 + nrm(ks[30], (L, D_MODEL), 0.02)
    inp['ffn2_wi'] = nrm(ks[31], (L, D_MODEL, 2 * FFN_DIM), D_MODEL ** -0.5)
    inp['ffn2_wo'] = nrm(ks[32], (L, FFN_DIM, D_MODEL), FFN_DIM ** -0.5)
    inp['final_norm'] = 1.0 + nrm(ks[33], (D_MODEL,), 0.02)
    return inp


def reference(x_prompt, x_sample, cache_k, cache_v, state_rwkv, state_shift,
              ffn1_norm, ffn1_wi, ffn1_wo, mix_norm, w_in,
              lambda_q1, lambda_k1, lambda_q2, lambda_k2, subln_g,
              shift_mu, rw_w0, rw_w2, rw_a0, rw_a2, rw_g2, rw_k_k, rw_k_a, rw_r_k, rw_lnx_g, rw_lnx_b,
              w_proj_a, w_proj_b, w_out, ffn2_norm, ffn2_wi, ffn2_wo, final_norm):
    def params(l):
        return dict(ffn1_norm=ffn1_norm[l], ffn1_wi=ffn1_wi[l], ffn1_wo=ffn1_wo[l],
                    mix_norm=mix_norm[l], w_in=w_in[l],
                    lambda_q1=lambda_q1[l], lambda_k1=lambda_k1[l],
                    lambda_q2=lambda_q2[l], lambda_k2=lambda_k2[l], subln_g=subln_g[l],
                    shift_mu=shift_mu[l], rw_w0=rw_w0[l], rw_w2=rw_w2[l], rw_a0=rw_a0[l], rw_a2=rw_a2[l],
                    rw_g2=rw_g2[l], rw_k_k=rw_k_k[l], rw_k_a=rw_k_a[l], rw_r_k=rw_r_k[l],
                    rw_lnx_g=rw_lnx_g[l], rw_lnx_b=rw_lnx_b[l],
                    w_proj_a=w_proj_a[l], w_proj_b=w_proj_b[l], w_out=w_out[l],
                    ffn2_norm=ffn2_norm[l], ffn2_wi=ffn2_wi[l], ffn2_wo=ffn2_wo[l])

    xp = x_prompt
    kp, vp, sp, shp = [], [], [], []
    for l in range(DEPTH):
        xp, k_l, v_l, s_l, sh_l = layer(xp, l, None, None, None, None, params(l))
        kp.append(k_l); vp.append(v_l); sp.append(s_l); shp.append(sh_l)
    y_prompt = rmsnorm(xp, final_norm)

    xs = x_sample
    ks_, vs_, ss_, shs_ = [], [], [], []
    for l in range(DEPTH):
        xs, k_l, v_l, s_l, sh_l = layer(xs, l, cache_k[l], cache_v[l], state_rwkv[l], state_shift[l], params(l))
        ks_.append(k_l); vs_.append(v_l); ss_.append(s_l); shs_.append(sh_l)
    y_sample = rmsnorm(xs, final_norm)

    return (y_prompt, y_sample,
            jnp.stack(kp), jnp.stack(vp), jnp.stack(sp), jnp.stack(shp),
            jnp.stack(ks_), jnp.stack(vs_), jnp.stack(ss_), jnp.stack(shs_))
```

```python
import functools
import math

import jax
import jax.numpy as jnp
from jax import lax
from jax.experimental import pallas as pl
from jax.experimental.pallas import tpu as pltpu

F32 = jnp.float32
BF16 = jnp.bfloat16

D_MODEL = 2048
DA_HEADS = 8
DA_HD = 64
DA_WIDTH = DA_HEADS * 2 * DA_HD
RW_HEADS = 16
RW_HD = 64
RW_WIDTH = RW_HEADS * RW_HD
RW_DECAY_LORA = 96
RW_AAA_LORA = 96
RW_GATE_LORA = 256
RW_LORA = RW_DECAY_LORA + RW_AAA_LORA + RW_GATE_LORA
RW_LORA_PAD = 512
RW_PROJ = 3 * RW_WIDTH + RW_LORA
FFN_DIM = 11 * D_MODEL // 4
CHUNK = 64
NORM_EPS = 1e-6
SUBLN_EPS = 1e-5
LNX_EPS = 64e-5
NEG_BIG = -1e30

LANES = 128
PAIR = 2 * RW_HD
VMEM_LIMIT = 56 * 1024 * 1024

NT_DIMS = (((1,), (1,)), ((), ()))
TN_DIMS = (((0,), (0,)), ((), ()))


def _dot(a, b):
    return jnp.dot(a, b, preferred_element_type=F32)


def _dot_nt(a, b):
    return lax.dot_general(a, b, NT_DIMS, preferred_element_type=F32)


def _dot_tn(a, b):
    return lax.dot_general(a, b, TN_DIMS, preferred_element_type=F32)


def _bf(x):
    return x.astype(BF16)


def _rms(x, g, eps):
    return x * lax.rsqrt(jnp.mean(x * x, axis=-1, keepdims=True) + eps) * g


def _params(*sem):
    return pltpu.CompilerParams(dimension_semantics=sem, vmem_limit_bytes=VMEM_LIMIT)


def _const_spec(shape):
    nd = len(shape)
    return pl.BlockSpec(shape, lambda *_: (0,) * nd)


def _ffn_kernel(*refs, final):
    if final:
        x_ref, g_ref, wg_ref, wu_ref, wo_ref, fg_ref, o_ref, h_scr, acc_scr = refs
    else:
        x_ref, g_ref, wg_ref, wu_ref, wo_ref, o_ref, h_scr, acc_scr = refs
    f = pl.program_id(1)

    @pl.when(f == 0)
    def _():
        h_scr[...] = _bf(_rms(x_ref[...], g_ref[...], NORM_EPS))
        acc_scr[...] = jnp.zeros_like(acc_scr)

    h = h_scr[...]
    gate = _dot(h, wg_ref[...])
    up = _dot(h, wu_ref[...])
    act = _bf(gate * jax.nn.sigmoid(gate) * up)
    acc_scr[...] += _dot(act, wo_ref[...])

    @pl.when(f == pl.num_programs(1) - 1)
    def _():
        y = x_ref[...] + 0.5 * acc_scr[...]
        if final:
            y = _rms(y, fg_ref[...], NORM_EPS)
        o_ref[...] = y


def _ffn(x, g, wi, wo, final_g=None, *, tm=512, tf=512):
    n = x.shape[0]
    tm = min(tm, n)
    nf = FFN_DIM // tf
    final = final_g is not None
    in_specs = [
        pl.BlockSpec((tm, D_MODEL), lambda i, f: (i, 0)),
        _const_spec((1, D_MODEL)),
        pl.BlockSpec((D_MODEL, tf), lambda i, f: (0, f)),
        pl.BlockSpec((D_MODEL, tf), lambda i, f: (0, f + nf)),
        pl.BlockSpec((tf, D_MODEL), lambda i, f: (f, 0)),
    ]
    args = [x, g, wi, wi, wo]
    if final:
        in_specs.append(_const_spec((1, D_MODEL)))
        args.append(final_g)
    return pl.pallas_call(
        functools.partial(_ffn_kernel, final=final),
        out_shape=jax.ShapeDtypeStruct((n, D_MODEL), F32),
        grid=(n // tm, nf),
        in_specs=in_specs,
        out_specs=pl.BlockSpec((tm, D_MODEL), lambda i, f: (i, 0)),
        scratch_shapes=[pltpu.VMEM((tm, D_MODEL), BF16), pltpu.VMEM((tm, D_MODEL), F32)],
        compiler_params=_params("parallel", "arbitrary"),
        name="ffn_final" if final else "ffn",
    )(*args)


def _qkv_kernel(x_ref, g_ref, wq_ref, wk_ref, wv_ref, q_ref, k_ref, v_ref, kb_ref, vb_ref):
    h = _bf(_rms(x_ref[...], g_ref[...], NORM_EPS))
    q_ref[...] = _bf(_dot(h, wq_ref[...]) * (DA_HD ** -0.5))
    k = _dot(h, wk_ref[...])
    k_ref[...] = k
    kb_ref[...] = _bf(k)
    v = _dot(h, wv_ref[...])
    v_ref[...] = v
    vb_ref[...] = _bf(v)


def _qkv_proj(x, g, wq, wk, wv, *, tm=256):
    n = x.shape[0]
    tm = min(tm, n)
    row = lambda w: pl.BlockSpec((tm, w), lambda i: (i, 0))
    wspec = _const_spec((D_MODEL, DA_WIDTH))
    return pl.pallas_call(
        _qkv_kernel,
        out_shape=(jax.ShapeDtypeStruct((n, DA_WIDTH), BF16),
                   jax.ShapeDtypeStruct((n, DA_WIDTH), F32),
                   jax.ShapeDtypeStruct((n, DA_WIDTH), F32),
                   jax.ShapeDtypeStruct((n, DA_WIDTH), BF16),
                   jax.ShapeDtypeStruct((n, DA_WIDTH), BF16)),
        grid=(n // tm,),
        in_specs=[row(D_MODEL), _const_spec((1, D_MODEL)), wspec, wspec, wspec],
        out_specs=(row(DA_WIDTH),) * 5,
        compiler_params=_params("parallel"),
        name="qkv_proj",
    )(x, g, wq, wk, wv)


def _rw_proj_kernel(x_ref, g_ref, wrkv_ref, wl_ref, rkv_ref, lora_ref):
    h = _bf(_rms(x_ref[...], g_ref[...], NORM_EPS))
    rkv_ref[...] = _dot(h, wrkv_ref[...])
    lora_ref[...] = _dot(h, wl_ref[...])


def _rw_proj(x, g, wrkv, wl, *, tm=256):
    n = x.shape[0]
    tm = min(tm, n)
    row = lambda w: pl.BlockSpec((tm, w), lambda i: (i, 0))
    return pl.pallas_call(
        _rw_proj_kernel,
        out_shape=(jax.ShapeDtypeStruct((n, 3 * RW_WIDTH), F32),
                   jax.ShapeDtypeStruct((n, RW_LORA_PAD), F32)),
        grid=(n // tm,),
        in_specs=[row(D_MODEL), _const_spec((1, D_MODEL)),
                  _const_spec((D_MODEL, 3 * RW_WIDTH)), _const_spec((D_MODEL, RW_LORA_PAD))],
        out_specs=(row(3 * RW_WIDTH), row(RW_LORA_PAD)),
        compiler_params=_params("parallel"),
        name="rw_proj",
    )(x, g, wrkv, wl)


def _gate_kernel(x_ref, g_ref, wa_ref, wb_ref, ga_ref, gb_ref):
    h = _bf(_rms(x_ref[...], g_ref[...], NORM_EPS))
    ga_ref[...] = _bf(jax.nn.sigmoid(_dot(h, wa_ref[...])))
    gb_ref[...] = _bf(jax.nn.sigmoid(_dot(h, wb_ref[...])))


def _gate_proj(x, g, wa, wb, *, tm=256):
    n = x.shape[0]
    tm = min(tm, n)
    row = pl.BlockSpec((tm, D_MODEL), lambda i: (i, 0))
    wspec = _const_spec((D_MODEL, D_MODEL))
    return pl.pallas_call(
        _gate_kernel,
        out_shape=(jax.ShapeDtypeStruct((n, D_MODEL), BF16),) * 2,
        grid=(n // tm,),
        in_specs=[row, _const_spec((1, D_MODEL)), wspec, wspec],
        out_specs=(row, row),
        compiler_params=_params("parallel"),
        name="gate_proj",
    )(x, g, wa, wb)


def _stack_maps(q):
    lane = lax.broadcasted_iota(jnp.int32, q.shape, 1)
    zero = jnp.zeros_like(q)
    return jnp.concatenate([jnp.where(lane < DA_HD, q, zero), jnp.where(lane >= DA_HD, q, zero)], axis=0)


def _online_softmax_step(qst, kb, vb, m_scr, l_scr, acc_scr, mask=None):
    s = _dot_nt(qst, kb)
    if mask is not None:
        s = jnp.where(mask, s, NEG_BIG)
    m_prev = m_scr[...]
    m_next = jnp.maximum(m_prev, jnp.max(s, axis=1, keepdims=True))
    alpha = jnp.exp(m_prev - m_next)
    p = jnp.exp(s - m_next)
    l_scr[...] = alpha * l_scr[...] + jnp.sum(p, axis=1, keepdims=True)
    acc_scr[...] = alpha * acc_scr[...] + _dot(_bf(p), vb)
    m_scr[...] = m_next


def _attn_init(m_scr, l_scr, acc_scr):
    m_scr[...] = jnp.full_like(m_scr, NEG_BIG)
    l_scr[...] = jnp.zeros_like(l_scr)
    acc_scr[...] = jnp.zeros_like(acc_scr)


def _attn_finalize(tq, lam_refs, sg_ref, o_ref, l_scr, acc_scr, lam_init):
    lq1, lk1, lq2, lk2 = lam_refs
    lam = (jnp.exp(jnp.sum(lq1[...] * lk1[...], axis=-1, keepdims=True))
           - jnp.exp(jnp.sum(lq2[...] * lk2[...], axis=-1, keepdims=True)) + lam_init)
    o = acc_scr[...] / l_scr[...]
    o = o[:tq] - lam * o[tq:]
    o = o * lax.rsqrt(jnp.mean(o * o, axis=-1, keepdims=True) + SUBLN_EPS)
    o_ref[...] = (o * sg_ref[...] * (1.0 - lam_init)).astype(o_ref.dtype)


def _attn_prompt_kernel(q_ref, k_ref, v_ref, lq1, lk1, lq2, lk2, sg_ref, o_ref,
                        qst_scr, m_scr, l_scr, acc_scr, *, tq, lam_init):
    qi = pl.program_id(2)
    qst_scr[...] = _stack_maps(q_ref[...])
    _attn_init(m_scr, l_scr, acc_scr)

    def kv(j):
        start = pl.multiple_of(j * tq, tq)
        return k_ref[pl.ds(start, tq), :], v_ref[pl.ds(start, tq), :]

    def body(j, carry):
        kb, vb = kv(j)
        _online_softmax_step(qst_scr[...], kb, vb, m_scr, l_scr, acc_scr)
        return carry

    lax.fori_loop(0, qi, body, 0)
    qc = (lax.broadcasted_iota(jnp.int32, (2 * tq, tq), 0) % tq) // CHUNK
    kc = lax.broadcasted_iota(jnp.int32, (2 * tq, tq), 1) // CHUNK
    kb, vb = kv(qi)
    _online_softmax_step(qst_scr[...], kb, vb, m_scr, l_scr, acc_scr, mask=kc <= qc)
    _attn_finalize(tq, (lq1, lk1, lq2, lk2), sg_ref, o_ref, l_scr, acc_scr, lam_init)


def _attn_scratch(tq):
    return [pltpu.VMEM((2 * tq, LANES), BF16), pltpu.VMEM((2 * tq, 1), F32),
            pltpu.VMEM((2 * tq, 1), F32), pltpu.VMEM((2 * tq, LANES), F32)]


def _attn_prompt(q, kb, vb, lam_vecs, subln_g, nb, seq, lam_init, *, tq=512):
    tq = min(tq, seq)
    nq = seq // tq
    qspec = pl.BlockSpec((tq, LANES), lambda b, h, i: (b * nq + i, h))
    kvspec = pl.BlockSpec((seq, LANES), lambda b, h, i: (b, h))
    vec = _const_spec((1, DA_HD))
    return pl.pallas_call(
        functools.partial(_attn_prompt_kernel, tq=tq, lam_init=lam_init),
        out_shape=jax.ShapeDtypeStruct((nb * seq, DA_WIDTH), BF16),
        grid=(nb, DA_HEADS, nq),
        in_specs=[qspec, kvspec, kvspec, vec, vec, vec, vec, _const_spec((1, LANES))],
        out_specs=qspec,
        scratch_shapes=_attn_scratch(tq),
        compiler_params=_params("parallel", "parallel", "arbitrary"),
        name="attn_prompt",
    )(q, kb, vb, *lam_vecs, subln_g)


def _attn_sample_kernel(q_ref, kn_ref, vn_ref, kc_ref, vc_ref, lq1, lk1, lq2, lk2, sg_ref, o_ref,
                        qst_scr, m_scr, l_scr, acc_scr, *, tq, lam_init):
    j = pl.program_id(2)

    @pl.when(j == 0)
    def _():
        qst_scr[...] = _stack_maps(q_ref[...])
        _attn_init(m_scr, l_scr, acc_scr)
        _online_softmax_step(qst_scr[...], kn_ref[...], vn_ref[...], m_scr, l_scr, acc_scr)

    _online_softmax_step(qst_scr[...], _bf(kc_ref[...]), _bf(vc_ref[...]), m_scr, l_scr, acc_scr)

    @pl.when(j == pl.num_programs(2) - 1)
    def _():
        _attn_finalize(tq, (lq1, lk1, lq2, lk2), sg_ref, o_ref, l_scr, acc_scr, lam_init)


def _attn_sample(q, kb, vb, cache_k, cache_v, lam_vecs, subln_g, nb, seq, lam_init, *, tk=2048):
    past = cache_k.shape[1]
    tk = min(tk, past)
    qspec = pl.BlockSpec((seq, LANES), lambda b, h, j: (b, h))
    cspec = pl.BlockSpec((None, tk, LANES), lambda b, h, j: (b, j, h))
    vec = _const_spec((1, DA_HD))
    return pl.pallas_call(
        functools.partial(_attn_sample_kernel, tq=seq, lam_init=lam_init),
        out_shape=jax.ShapeDtypeStruct((nb * seq, DA_WIDTH), BF16),
        grid=(nb, DA_HEADS, past // tk),
        in_specs=[qspec, qspec, qspec, cspec, cspec, vec, vec, vec, vec, _const_spec((1, LANES))],
        out_specs=qspec,
        scratch_shapes=_attn_scratch(seq),
        compiler_params=_params("parallel", "parallel", "arbitrary"),
        name="attn_sample",
    )(q, kb, vb, cache_k, cache_v, *lam_vecs, subln_g)


def _split3(x):
    hi = _bf(x)
    r1 = x - hi.astype(F32)
    mid = _bf(r1)
    lo = _bf(r1 - mid.astype(F32))
    return hi, mid, lo


def _seg_sum(x, ones_bd):
    hi = _bf(x)
    lo = _bf(x - hi.astype(F32))
    return _dot(hi, ones_bd) + _dot(lo, ones_bd)


def _stack_heads(x):
    lane = lax.broadcasted_iota(jnp.int32, x.shape, 1)
    zero = jnp.zeros_like(x)
    return _bf(jnp.concatenate([jnp.where(lane < RW_HD, x, zero), jnp.where(lane >= RW_HD, x, zero)], axis=0))


def _unit_lower_inverse(a_ab, eye, blk16, blk32):
    a16 = jnp.where(blk16, a_ab, 0.0)
    a32 = jnp.where(blk32, a_ab, 0.0) - a16
    a64 = a_ab - a16 - a32
    t = eye - a16
    q = a16
    for _ in range(3):
        qb = _bf(q)
        q = _dot(qb, qb)
        t = t + _dot(_bf(t), _bf(q))
    for off in (a32, a64):
        tb = _bf(t)
        t = t - _dot(tb, _bf(_dot(_bf(off), tb)))
    return t


def _rwkv_kernel(rkv_ref, lora_ref, prkv_ref, plora_ref, s0_ref, mu_rkv_ref, mu_lora_ref, w2_ref,
                 w0_ref, a0_ref, kk_ref, ka_ref, rk_ref, lg_ref, lb_ref,
                 ob_ref, sout_ref, s_scr, prev_rkv_scr, prev_lora_scr):
    c = pl.program_id(1)

    @pl.when(c == 0)
    def _():
        s_scr[...] = s0_ref[...]
        prev_rkv_scr[...] = prkv_ref[...]
        prev_lora_scr[...] = plora_ref[...]

    row = lax.broadcasted_iota(jnp.int32, (CHUNK, 1), 0)

    def token_shift(u, prev, mu):
        u_prev = jnp.where(row == 0, prev, pltpu.roll(u, 1, 0))
        return u + mu * (u_prev - u)

    u = rkv_ref[...]
    ul = lora_ref[...]
    us = token_shift(u, prev_rkv_scr[...], mu_rkv_ref[...])
    usl = token_shift(ul, prev_lora_scr[...], mu_lora_ref[...])
    prev_rkv_scr[...] = u[CHUNK - 1:CHUNK, :]
    prev_lora_scr[...] = ul[CHUNK - 1:CHUNK, :]

    lane_l = lax.broadcasted_iota(jnp.int32, usl.shape, 1)
    act = jnp.where(lane_l < RW_DECAY_LORA, jnp.tanh(usl),
                    jnp.where(lane_l < RW_DECAY_LORA + RW_AAA_LORA, usl, jax.nn.sigmoid(usl)))
    lin = _dot(_bf(act), w2_ref[...])

    r = us[:, 0:RW_WIDTH]
    k = us[:, RW_WIDTH:2 * RW_WIDTH]
    v = us[:, 2 * RW_WIDTH:3 * RW_WIDTH]
    wx = -(w0_ref[...] + lin[:, 0:RW_WIDTH])
    w = -(jnp.maximum(wx, 0.0) + jnp.log(1.0 + jnp.exp(-jnp.abs(wx)))) - 0.5
    log_decay = -jnp.exp(w)
    a = jax.nn.sigmoid(a0_ref[...] + lin[:, RW_WIDTH:2 * RW_WIDTH])
    g = lin[:, 2 * RW_WIDTH:3 * RW_WIDTH]

    ti = lax.broadcasted_iota(jnp.int32, (CHUNK, CHUNK), 0)
    si = lax.broadcasted_iota(jnp.int32, (CHUNK, CHUNK), 1)
    tri = jnp.where(ti >= si, 1.0, 0.0).astype(BF16)
    cs = sum(_dot(tri, part) for part in _split3(log_decay))

    ri = lax.broadcasted_iota(jnp.int32, (PAIR, PAIR), 0)
    ci = lax.broadcasted_iota(jnp.int32, (PAIR, PAIR), 1)
    eye = jnp.where(ri == ci, 1.0, 0.0).astype(F32)
    same_head = (ri // RW_HD) == (ci // RW_HD)
    ones_bd = jnp.where(same_head, 1.0, 0.0).astype(BF16)
    strict = same_head & (ri > ci)
    incl = same_head & (ri >= ci)
    blk16 = (ri // 16) == (ci // 16)
    blk32 = (ri // 32) == (ci // 32)

    for p in range(RW_HEADS // 2):
        sl = slice(p * PAIR, (p + 1) * PAIR)
        kk = k[:, sl] * kk_ref[:, sl]
        kk = kk * lax.rsqrt(jnp.maximum(_seg_sum(kk * kk, ones_bd), 1e-24))
        a_p = a[:, sl]
        k_p = k[:, sl] * (1.0 + (a_p - 1.0) * ka_ref[:, sl])
        r_p = r[:, sl]
        v_p = v[:, sl]
        cs_p = cs[:, sl]
        g_fwd = jnp.exp(cs_p)
        g_inv = jnp.exp(-cs_p)
        g_prev = jnp.exp(cs_p - log_decay[:, sl])

        ka_s = _stack_heads(kk * g_prev)
        b_s = _stack_heads(kk * a_p * g_inv)
        k_s = _stack_heads(k_p * g_inv)
        r_s = _stack_heads(r_p * g_fwd)
        v_s = _stack_heads(v_p)

        a_all = _dot_nt(jnp.concatenate([ka_s, r_s], axis=0), jnp.concatenate([b_s, k_s], axis=0))
        a_ab = jnp.where(strict, a_all[:PAIR, :PAIR], 0.0)
        a_ak = jnp.where(strict, a_all[:PAIR, PAIR:], 0.0)
        a_rb = jnp.where(incl, a_all[PAIR:, :PAIR], 0.0)
        a_rk = jnp.where(incl, a_all[PAIR:, PAIR:], 0.0)
        t_inv = _unit_lower_inverse(a_ab, eye, blk16, blk32)

        s_old = s_scr[p]
        s_b = _bf(s_old)
        sa = _dot(_bf(t_inv), _bf(-(_dot_nt(ka_s, s_b) + _dot(_bf(a_ak), v_s))))
        sa_b = _bf(sa)
        y_bd = _dot_nt(r_s, s_b) + _dot(_bf(a_rb), sa_b) + _dot(_bf(a_rk), v_s)
        y = y_bd[:CHUNK] + y_bd[CHUNK:]
        upd = _dot_tn(jnp.concatenate([sa_b, v_s], axis=0), jnp.concatenate([b_s, k_s], axis=0))
        s_scr[p] = g_fwd[CHUNK - 1:CHUNK, :] * (s_old + upd)

        mean = _seg_sum(y, ones_bd) * (1.0 / RW_HD)
        yc = y - mean
        var = _seg_sum(yc * yc, ones_bd) * (1.0 / RW_HD)
        yn = yc * lax.rsqrt(var + LNX_EPS) * lg_ref[:, sl] + lb_ref[:, sl]
        bonus = _seg_sum(r_p * k_p * rk_ref[:, sl], ones_bd) * v_p
        ob_ref[:, sl] = ((yn + bonus) * g[:, sl]).astype(ob_ref.dtype)

    @pl.when(c == pl.num_programs(1) - 1)
    def _():
        sout_ref[...] = s_scr[...]


def _rwkv(rkv, lora, prev_rkv, prev_lora, s0_bd, w, nb, seq):
    nc = seq // CHUNK
    npair = RW_HEADS // 2
    tok = lambda width: pl.BlockSpec((CHUNK, width), lambda b, c: (b * nc + c, 0))
    per_seq = lambda width: pl.BlockSpec((None, 1, width), lambda b, c: (b, 0, 0))
    state = pl.BlockSpec((None, npair, PAIR, PAIR), lambda b, c: (b, 0, 0, 0))
    vec = _const_spec((1, RW_WIDTH))
    return pl.pallas_call(
        _rwkv_kernel,
        out_shape=(jax.ShapeDtypeStruct((nb * seq, RW_WIDTH), BF16),
                   jax.ShapeDtypeStruct((nb, npair, PAIR, PAIR), F32)),
        grid=(nb, nc),
        in_specs=[tok(3 * RW_WIDTH), tok(RW_LORA_PAD), per_seq(3 * RW_WIDTH), per_seq(RW_LORA_PAD), state,
                  _const_spec((1, 3 * RW_WIDTH)), _const_spec((1, RW_LORA_PAD)),
                  _const_spec((RW_LORA_PAD, 3 * RW_WIDTH)), vec, vec, vec, vec, vec, vec, vec],
        out_specs=(tok(RW_WIDTH), state),
        scratch_shapes=[pltpu.VMEM((npair, PAIR, PAIR), F32), pltpu.VMEM((1, 3 * RW_WIDTH), F32),
                        pltpu.VMEM((1, RW_LORA_PAD), F32)],
        compiler_params=_params("parallel", "arbitrary"),
        name="rwkv",
    )(rkv, lora, prev_rkv, prev_lora, s0_bd, w["mu_rkv"], w["mu_lora"], w["w2cat"],
      w["w0"], w["a0"], w["k_k"], w["k_a"], w["r_k"], w["lnx_g"], w["lnx_b"])


def _merge_kernel(x_ref, oa_ref, ob_ref, ga_ref, gb_ref, wa_ref, wb_ref, wo_ref, o_ref):
    merged = (ga_ref[...].astype(F32) * _dot(oa_ref[...], wa_ref[...])
              + gb_ref[...].astype(F32) * _dot(ob_ref[...], wb_ref[...]))
    o_ref[...] = x_ref[...] + _dot(_bf(merged), wo_ref[...])


def _merge(x, oa, ob, ga, gb, wa, wb, wo, *, tm=256):
    n = x.shape[0]
    tm = min(tm, n)
    row = lambda w: pl.BlockSpec((tm, w), lambda i: (i, 0))
    return pl.pallas_call(
        _merge_kernel,
        out_shape=jax.ShapeDtypeStruct((n, D_MODEL), F32),
        grid=(n // tm,),
        in_specs=[row(D_MODEL), row(DA_WIDTH), row(RW_WIDTH), row(D_MODEL), row(D_MODEL),
                  _const_spec((DA_WIDTH, D_MODEL)), _const_spec((RW_WIDTH, D_MODEL)),
                  _const_spec((D_MODEL, D_MODEL))],
        out_specs=row(D_MODEL),
        compiler_params=_params("parallel"),
        name="merge",
    )(x, oa, ob, ga, gb, wa, wb, wo)


def _prep_weights(l, p):
    row = lambda v: v.reshape(1, -1).astype(F32)
    w_in = p["w_in"][l]
    o_rw = 3 * DA_WIDTH
    o_g = o_rw + RW_PROJ
    w_lora = jnp.pad(w_in[:, o_rw + 3 * RW_WIDTH:o_g], ((0, 0), (0, RW_LORA_PAD - RW_LORA)))
    w2cat = jnp.zeros((RW_LORA_PAD, 3 * RW_WIDTH), F32)
    w2cat = w2cat.at[0:RW_DECAY_LORA, 0:RW_WIDTH].set(p["rw_w2"][l])
    w2cat = w2cat.at[RW_DECAY_LORA:RW_DECAY_LORA + RW_AAA_LORA, RW_WIDTH:2 * RW_WIDTH].set(p["rw_a2"][l])
    w2cat = w2cat.at[RW_DECAY_LORA + RW_AAA_LORA:RW_LORA, 2 * RW_WIDTH:].set(p["rw_g2"][l])
    mu = p["shift_mu"][l]
    return dict(
        ffn1_norm=row(p["ffn1_norm"][l]), ffn1_wi=_bf(p["ffn1_wi"][l]), ffn1_wo=_bf(p["ffn1_wo"][l]),
        ffn2_norm=row(p["ffn2_norm"][l]), ffn2_wi=_bf(p["ffn2_wi"][l]), ffn2_wo=_bf(p["ffn2_wo"][l]),
        mix_norm=row(p["mix_norm"][l]),
        wq=_bf(w_in[:, 0:DA_WIDTH]), wk=_bf(w_in[:, DA_WIDTH:2 * DA_WIDTH]), wv=_bf(w_in[:, 2 * DA_WIDTH:o_rw]),
        w_rkv=_bf(w_in[:, o_rw:o_rw + 3 * RW_WIDTH]), w_lora=_bf(w_lora),
        w_ga=_bf(w_in[:, o_g:o_g + D_MODEL]), w_gb=_bf(w_in[:, o_g + D_MODEL:]),
        lam_vecs=tuple(row(p[n][l]) for n in ("lambda_q1", "lambda_k1", "lambda_q2", "lambda_k2")),
        subln_g=row(p["subln_g"][l]),
        mu_rkv=row(mu[:3 * RW_WIDTH]), mu_lora=row(jnp.pad(mu[3 * RW_WIDTH:], (0, RW_LORA_PAD - RW_LORA))),
        w2cat=_bf(w2cat), w0=row(p["rw_w0"][l]), a0=row(p["rw_a0"][l]), k_k=row(p["rw_k_k"][l]),
        k_a=row(p["rw_k_a"][l]), r_k=row(p["rw_r_k"][l]), lnx_g=row(p["rw_lnx_g"][l]), lnx_b=row(p["rw_lnx_b"][l]),
        w_proj_a=_bf(p["w_proj_a"][l]), w_proj_b=_bf(p["w_proj_b"][l]), w_out=_bf(p["w_out"][l]),
    )


def _state_to_pairs(s):
    nb = s.shape[0]
    s = s.reshape(nb, RW_HEADS // 2, 2, RW_HD, 1, RW_HD)
    sel = jnp.eye(2, dtype=s.dtype).reshape(1, 1, 2, 1, 2, 1)
    return (s * sel).reshape(nb, RW_HEADS // 2, PAIR, PAIR)


def _pairs_to_state(s):
    nb = s.shape[0]
    s = s.reshape(nb, RW_HEADS // 2, 2, RW_HD, 2, RW_HD)
    return jnp.stack([s[:, :, 0, :, 0, :], s[:, :, 1, :, 1, :]], axis=2).reshape(nb, RW_HEADS, RW_HD, RW_HD)


def _layer(x, l, nb, seq, w, cache_k, cache_v, s0, prev, final_g):
    lam_init = 0.8 - 0.6 * math.exp(-0.3 * l)
    x = _ffn(x, w["ffn1_norm"], w["ffn1_wi"], w["ffn1_wo"])
    q, k, v, kb, vb = _qkv_proj(x, w["mix_norm"], w["wq"], w["wk"], w["wv"])
    rkv, lora = _rw_proj(x, w["mix_norm"], w["w_rkv"], w["w_lora"])
    ga, gb = _gate_proj(x, w["mix_norm"], w["w_ga"], w["w_gb"])
    if cache_k is None:
        o_a = _attn_prompt(q, kb, vb, w["lam_vecs"], w["subln_g"], nb, seq, lam_init)
        s0 = jnp.zeros((nb, RW_HEADS, RW_HD, RW_HD), F32)
        prev = jnp.zeros((nb, 1, RW_PROJ), F32)
    else:
        past = cache_k.shape[1]
        o_a = _attn_sample(q, kb, vb, cache_k.reshape(nb, past, DA_WIDTH), cache_v.reshape(nb, past, DA_WIDTH),
                           w["lam_vecs"], w["subln_g"], nb, seq, lam_init)
    prev_rkv = prev[:, :, :3 * RW_WIDTH]
    prev_lora = jnp.pad(prev[:, :, 3 * RW_WIDTH:], ((0, 0), (0, 0), (0, RW_LORA_PAD - RW_LORA)))
    o_b, s_new = _rwkv(rkv, lora, prev_rkv, prev_lora, _state_to_pairs(s0), w, nb, seq)
    x = _merge(x, o_a, o_b, ga, gb, w["w_proj_a"], w["w_proj_b"], w["w_out"])
    x = _ffn(x, w["ffn2_norm"], w["ffn2_wi"], w["ffn2_wo"], final_g)
    last = jnp.concatenate([rkv.reshape(nb, seq, -1)[:, -1:], lora.reshape(nb, seq, -1)[:, -1:, :RW_LORA]], axis=-1)
    return (x, k.reshape(nb, seq, DA_HEADS, 2 * DA_HD), v.reshape(nb, seq, DA_HEADS, 2 * DA_HD),
            _pairs_to_state(s_new), last)


def _run_stream(x, weights, final_norm, caches):
    nb, seq, _ = x.shape
    x = x.reshape(nb * seq, D_MODEL)
    ks, vs, ss, shs = [], [], [], []
    depth = len(weights)
    for l, w in enumerate(weights):
        final_g = final_norm.reshape(1, -1) if l == depth - 1 else None
        if caches is None:
            ck = cv = s0 = prev = None
        else:
            ck, cv, s0, prev = (c[l] for c in caches)
        x, k, v, s, sh = _layer(x, l, nb, seq, w, ck, cv, s0, prev, final_g)
        ks.append(k); vs.append(v); ss.append(s); shs.append(sh)
    return x.reshape(nb, seq, D_MODEL), jnp.stack(ks), jnp.stack(vs), jnp.stack(ss), jnp.stack(shs)


def kernel(x_prompt, x_sample, cache_k, cache_v, state_rwkv, state_shift, ffn1_norm, ffn1_wi, ffn1_wo, mix_norm, w_in, lambda_q1, lambda_k1, lambda_q2, lambda_k2, subln_g, shift_mu, rw_w0, rw_w2, rw_a0, rw_a2, rw_g2, rw_k_k, rw_k_a, rw_r_k, rw_lnx_g, rw_lnx_b, w_proj_a, w_proj_b, w_out, ffn2_norm, ffn2_wi, ffn2_wo, final_norm):
    params = dict(ffn1_norm=ffn1_norm, ffn1_wi=ffn1_wi, ffn1_wo=ffn1_wo, mix_norm=mix_norm, w_in=w_in,
                  lambda_q1=lambda_q1, lambda_k1=lambda_k1, lambda_q2=lambda_q2, lambda_k2=lambda_k2,
                  subln_g=subln_g, shift_mu=shift_mu, rw_w0=rw_w0, rw_w2=rw_w2, rw_a0=rw_a0, rw_a2=rw_a2,
                  rw_g2=rw_g2, rw_k_k=rw_k_k, rw_k_a=rw_k_a, rw_r_k=rw_r_k, rw_lnx_g=rw_lnx_g, rw_lnx_b=rw_lnx_b,
                  w_proj_a=w_proj_a, w_proj_b=w_proj_b, w_out=w_out, ffn2_norm=ffn2_norm, ffn2_wi=ffn2_wi,
                  ffn2_wo=ffn2_wo)
    weights = [_prep_weights(l, params) for l in range(ffn1_wi.shape[0])]
    y_p, k_p, v_p, s_p, sh_p = _run_stream(x_prompt, weights, final_norm, None)
    y_s, k_s, v_s, s_s, sh_s = _run_stream(x_sample, weights, final_norm,
                                           (cache_k, cache_v, state_rwkv, state_shift))
    return (y_p, y_s, k_p, v_p, s_p, sh_p, k_s, v_s, s_s, sh_s)
```

```python
import functools
import math

import jax
import jax.numpy as jnp
from jax import lax
from jax.experimental import pallas as pl
from jax.experimental.pallas import tpu as pltpu

F32 = jnp.float32
BF16 = jnp.bfloat16

D_MODEL = 2048
DA_HEADS = 8
DA_HD = 64
DA_WIDTH = DA_HEADS * 2 * DA_HD
RW_HEADS = 16
RW_HD = 64
RW_WIDTH = RW_HEADS * RW_HD
RW_DECAY_LORA = 96
RW_AAA_LORA = 96
RW_GATE_LORA = 256
RW_LORA = RW_DECAY_LORA + RW_AAA_LORA + RW_GATE_LORA
RW_LORA_PAD = 512
RW_PROJ = 3 * RW_WIDTH + RW_LORA
FFN_DIM = 11 * D_MODEL // 4
CHUNK = 64
NORM_EPS = 1e-6
SUBLN_EPS = 1e-5
LNX_EPS = 64e-5
NEG_BIG = -1e30

LANES = 128
BF16_ROWS = 16
PAIR = 2 * RW_HD
VMEM_LIMIT = 56 * 1024 * 1024

NT_DIMS = (((1,), (1,)), ((), ()))
TN_DIMS = (((0,), (0,)), ((), ()))


def _dot(a, b):
    return jnp.dot(a, b, preferred_element_type=F32)


def _dot_nt(a, b):
    return lax.dot_general(a, b, NT_DIMS, preferred_element_type=F32)


def _dot_tn(a, b):
    return lax.dot_general(a, b, TN_DIMS, preferred_element_type=F32)


def _bf(x):
    return x.astype(BF16)


def _rms(x, g, eps):
    return x * lax.rsqrt(jnp.mean(x * x, axis=-1, keepdims=True) + eps) * g


def _params(*sem):
    return pltpu.CompilerParams(dimension_semantics=sem, vmem_limit_bytes=VMEM_LIMIT)


def _const_spec(shape):
    nd = len(shape)
    return pl.BlockSpec(shape, lambda *_: (0,) * nd, pipeline_mode=pl.Buffered(1))


def _ffn_kernel(*refs, final):
    if final:
        x_ref, g_ref, wg_ref, wu_ref, wo_ref, fg_ref, o_ref, h_scr, acc_scr = refs
    else:
        x_ref, g_ref, wg_ref, wu_ref, wo_ref, o_ref, h_scr, acc_scr = refs
    f = pl.program_id(1)

    @pl.when(f == 0)
    def _():
        h_scr[...] = _bf(_rms(x_ref[...], g_ref[...], NORM_EPS))
        acc_scr[...] = jnp.zeros_like(acc_scr)

    h = h_scr[...]
    gate = _dot(h, wg_ref[...])
    up = _dot(h, wu_ref[...])
    act = _bf(gate * jax.nn.sigmoid(gate) * up)
    acc_scr[...] += _dot(act, wo_ref[...])

    @pl.when(f == pl.num_programs(1) - 1)
    def _():
        y = x_ref[...] + 0.5 * acc_scr[...]
        if final:
            y = _rms(y, fg_ref[...], NORM_EPS)
        o_ref[...] = y


def _ffn(x, g, wi, wo, final_g=None, *, tm=512, tf=512):
    n = x.shape[0]
    tm = min(tm, n)
    nf = FFN_DIM // tf
    final = final_g is not None
    in_specs = [
        pl.BlockSpec((tm, D_MODEL), lambda i, f: (i, 0)),
        _const_spec((1, D_MODEL)),
        pl.BlockSpec((D_MODEL, tf), lambda i, f: (0, f)),
        pl.BlockSpec((D_MODEL, tf), lambda i, f: (0, f + nf)),
        pl.BlockSpec((tf, D_MODEL), lambda i, f: (f, 0)),
    ]
    args = [x, g, wi, wi, wo]
    if final:
        in_specs.append(_const_spec((1, D_MODEL)))
        args.append(final_g)
    return pl.pallas_call(
        functools.partial(_ffn_kernel, final=final),
        out_shape=jax.ShapeDtypeStruct((n, D_MODEL), F32),
        grid=(n // tm, nf),
        in_specs=in_specs,
        out_specs=pl.BlockSpec((tm, D_MODEL), lambda i, f: (i, 0)),
        scratch_shapes=[pltpu.VMEM((tm, D_MODEL), BF16), pltpu.VMEM((tm, D_MODEL), F32)],
        compiler_params=_params("parallel", "arbitrary"),
        name="ffn_final" if final else "ffn",
    )(*args)


def _qkv_kernel(x_ref, g_ref, wq_ref, wk_ref, wv_ref, wvt_ref, q_ref, k_ref, v_ref, kb_ref, vb_ref, *, transposed):
    h = _bf(_rms(x_ref[...], g_ref[...], NORM_EPS))
    q = _dot_nt(wq_ref[...], h) if transposed else _dot(h, wq_ref[...])
    q_ref[...] = _bf(q * (DA_HD ** -0.5))
    k = _dot(h, wk_ref[...])
    k_ref[...] = k
    kb_ref[...] = _bf(k)
    v = _dot(h, wv_ref[...])
    v_ref[...] = v
    vb_ref[...] = _bf(_dot_nt(wvt_ref[...], h)) if transposed else _bf(v)


def _qkv_proj(x, g, wq, wk, wv, wvt, *, transposed, tm):
    n = x.shape[0]
    row = lambda w: pl.BlockSpec((tm, w), lambda i: (i, 0))
    rows = lambda w, dt: jax.ShapeDtypeStruct((n, w), dt)
    wspec = _const_spec(wk.shape)
    if transposed:
        slab = pl.BlockSpec((None, DA_WIDTH, tm), lambda i: (i, 0, 0))
        slabs = jax.ShapeDtypeStruct((n // tm, DA_WIDTH, tm), BF16)
        q_spec, q_shape, vb_spec, vb_shape = slab, slabs, slab, slabs
    else:
        q_spec, q_shape, vb_spec, vb_shape = row(DA_WIDTH), rows(DA_WIDTH, BF16), row(DA_WIDTH), rows(DA_WIDTH, BF16)
    return pl.pallas_call(
        functools.partial(_qkv_kernel, transposed=transposed),
        out_shape=(q_shape, rows(DA_WIDTH, F32), rows(DA_WIDTH, F32), rows(DA_WIDTH, BF16), vb_shape),
        grid=(n // tm,),
        in_specs=[row(D_MODEL), _const_spec((1, D_MODEL)), _const_spec(wq.shape), wspec, wspec,
                  _const_spec(wvt.shape)],
        out_specs=(q_spec, row(DA_WIDTH), row(DA_WIDTH), row(DA_WIDTH), vb_spec),
        compiler_params=_params("parallel"),
        name="qkv_proj_t" if transposed else "qkv_proj",
    )(x, g, wq, wk, wv, wvt)


def _rw_proj_kernel(x_ref, g_ref, wrkv_ref, wl_ref, rkv_ref, lora_ref):
    h = _bf(_rms(x_ref[...], g_ref[...], NORM_EPS))
    rkv_ref[...] = _dot(h, wrkv_ref[...])
    lora_ref[...] = _dot(h, wl_ref[...])


def _rw_proj(x, g, wrkv, wl, *, tm=256):
    n = x.shape[0]
    tm = min(tm, n)
    row = lambda w: pl.BlockSpec((tm, w), lambda i: (i, 0))
    return pl.pallas_call(
        _rw_proj_kernel,
        out_shape=(jax.ShapeDtypeStruct((n, 3 * RW_WIDTH), F32),
                   jax.ShapeDtypeStruct((n, RW_LORA_PAD), F32)),
        grid=(n // tm,),
        in_specs=[row(D_MODEL), _const_spec((1, D_MODEL)),
                  _const_spec((D_MODEL, 3 * RW_WIDTH)), _const_spec((D_MODEL, RW_LORA_PAD))],
        out_specs=(row(3 * RW_WIDTH), row(RW_LORA_PAD)),
        compiler_params=_params("parallel"),
        name="rw_proj",
    )(x, g, wrkv, wl)


def _gate_kernel(x_ref, g_ref, wa_ref, wb_ref, ga_ref, gb_ref):
    h = _bf(_rms(x_ref[...], g_ref[...], NORM_EPS))
    ga_ref[...] = _bf(jax.nn.sigmoid(_dot(h, wa_ref[...])))
    gb_ref[...] = _bf(jax.nn.sigmoid(_dot(h, wb_ref[...])))


def _gate_proj(x, g, wa, wb, *, tm=256):
    n = x.shape[0]
    tm = min(tm, n)
    row = pl.BlockSpec((tm, D_MODEL), lambda i: (i, 0))
    wspec = _const_spec((D_MODEL, D_MODEL))
    return pl.pallas_call(
        _gate_kernel,
        out_shape=(jax.ShapeDtypeStruct((n, D_MODEL), BF16),) * 2,
        grid=(n // tm,),
        in_specs=[row, _const_spec((1, D_MODEL)), wspec, wspec],
        out_specs=(row, row),
        compiler_params=_params("parallel"),
        name="gate_proj",
    )(x, g, wa, wb)


def _stack_maps(q):
    lane = lax.broadcasted_iota(jnp.int32, q.shape, 1)
    zero = jnp.zeros_like(q)
    return jnp.concatenate([jnp.where(lane < DA_HD, q, zero), jnp.where(lane >= DA_HD, q, zero)], axis=0)


def _online_softmax_step(qst, kb, vb, m_scr, l_scr, acc_scr, mask=None):
    s = _dot_nt(qst, kb)
    if mask is not None:
        s = jnp.where(mask, s, NEG_BIG)
    m_prev = m_scr[...]
    m_next = jnp.maximum(m_prev, jnp.max(s, axis=1, keepdims=True))
    alpha = jnp.exp(m_prev - m_next)
    p = jnp.exp(s - m_next)
    l_scr[...] = alpha * l_scr[...] + jnp.sum(p, axis=1, keepdims=True)
    acc_scr[...] = alpha * acc_scr[...] + _dot(_bf(p), vb)
    m_scr[...] = m_next


def _attn_init(m_scr, l_scr, acc_scr):
    m_scr[...] = jnp.full_like(m_scr, NEG_BIG)
    l_scr[...] = jnp.zeros_like(l_scr)
    acc_scr[...] = jnp.zeros_like(acc_scr)


def _attn_finalize(tq, lam_refs, sg_ref, o_ref, l_scr, acc_scr, lam_init):
    lq1, lk1, lq2, lk2 = lam_refs
    lam = (jnp.exp(jnp.sum(lq1[...] * lk1[...], axis=-1, keepdims=True))
           - jnp.exp(jnp.sum(lq2[...] * lk2[...], axis=-1, keepdims=True)) + lam_init)
    o = acc_scr[...] / l_scr[...]
    o = o[:tq] - lam * o[tq:]
    o = o * lax.rsqrt(jnp.mean(o * o, axis=-1, keepdims=True) + SUBLN_EPS)
    o_ref[...] = (o * sg_ref[...] * (1.0 - lam_init)).astype(o_ref.dtype)


ATTN_STRIP = 256
ATTN_TQ = 512


def _attn_prompt_kernel(qt_ref, k_ref, vt_ref, lq1, lk1, lq2, lk2, sg_ref, o_ref,
                        qst_scr, m_scr, acc_scr, *, tq, lam_init):
    qi = pl.program_id(2)
    qt = qt_ref[...]
    feat = lax.broadcasted_iota(jnp.int32, qt.shape, 0)
    zero = jnp.zeros_like(qt)
    qst_scr[:, 0:tq] = jnp.where(feat < DA_HD, qt, zero)
    qst_scr[:, tq:2 * tq] = jnp.where(feat >= DA_HD, qt, zero)
    m_scr[...] = jnp.full_like(m_scr, NEG_BIG)
    acc_scr[...] = jnp.zeros_like(acc_scr)

    def block(j, masked):
        kb = k_ref[pl.ds(pl.multiple_of(j * tq, tq), tq), :]
        vt = jnp.concatenate([vt_ref[j], jnp.ones((BF16_ROWS, tq), BF16)], axis=0)
        n_strips = 2 * tq // ATTN_STRIP
        strip = lambda c: slice(c * ATTN_STRIP, (c + 1) * ATTN_STRIP)

        def scores(c):
            s = _dot(kb, qst_scr[:, strip(c)])
            if masked:
                kc = lax.broadcasted_iota(jnp.int32, s.shape, 0) // CHUNK
                qc = ((lax.broadcasted_iota(jnp.int32, s.shape, 1) + c * ATTN_STRIP) % tq) // CHUNK
                s = jnp.where(kc <= qc, s, NEG_BIG)
            return s

        s_next = scores(0)
        for c in range(n_strips):
            cs = strip(c)
            s = s_next
            if c + 1 < n_strips:
                s_next = scores(c + 1)
            m_prev = m_scr[:, cs]
            m_next = jnp.maximum(m_prev, jnp.max(s, axis=0, keepdims=True))
            alpha = jnp.exp(m_prev - m_next)
            p = jnp.exp(s - m_next)
            acc_scr[:, cs] = alpha * acc_scr[:, cs] + _dot(vt, _bf(p))
            m_scr[:, cs] = m_next

    def body(j, carry):
        block(j, False)
        return carry

    lax.fori_loop(0, qi, body, 0)
    block(qi, True)

    lam = (jnp.exp(jnp.sum(lq1[...] * lk1[...], axis=-1, keepdims=True))
           - jnp.exp(jnp.sum(lq2[...] * lk2[...], axis=-1, keepdims=True)) + lam_init)
    o = acc_scr[0:LANES, :] / acc_scr[LANES:LANES + 1, :]
    o = o[:, :tq] - lam * o[:, tq:]
    o = o * lax.rsqrt(jnp.mean(o * o, axis=0, keepdims=True) + SUBLN_EPS)
    o = o * sg_ref[...] * (1.0 - lam_init)
    o_ref[...] = o.T.astype(o_ref.dtype)


def _attn_scratch(tq):
    return [pltpu.VMEM((2 * tq, LANES), BF16), pltpu.VMEM((2 * tq, 1), F32),
            pltpu.VMEM((2 * tq, 1), F32), pltpu.VMEM((2 * tq, LANES), F32)]


def _attn_prompt(qt, kb, vt, lam_vecs, subln_g, nb, seq, lam_init, *, tq):
    nq = seq // tq
    vec = _const_spec((1, DA_HD))
    return pl.pallas_call(
        functools.partial(_attn_prompt_kernel, tq=tq, lam_init=lam_init),
        out_shape=jax.ShapeDtypeStruct((nb * seq, DA_WIDTH), BF16),
        grid=(nb, DA_HEADS, nq),
        in_specs=[pl.BlockSpec((None, LANES, tq), lambda b, h, i: (b * nq + i, h, 0)),
                  pl.BlockSpec((seq, LANES), lambda b, h, i: (b, h)),
                  pl.BlockSpec((nq, LANES, tq), lambda b, h, i: (b, h, 0)),
                  vec, vec, vec, vec, _const_spec((LANES, 1))],
        out_specs=pl.BlockSpec((tq, LANES), lambda b, h, i: (b * nq + i, h)),
        scratch_shapes=[pltpu.VMEM((LANES, 2 * tq), BF16), pltpu.VMEM((1, 2 * tq), F32),
                        pltpu.VMEM((LANES + BF16_ROWS, 2 * tq), F32)],
        compiler_params=_params("parallel", "parallel", "arbitrary"),
        name="attn_prompt",
    )(qt, kb, vt, *lam_vecs, subln_g.reshape(LANES, 1))


def _attn_sample_kernel(q_ref, kn_ref, vn_ref, kc_ref, vc_ref, lq1, lk1, lq2, lk2, sg_ref, o_ref,
                        qst_scr, m_scr, l_scr, acc_scr, *, tq, lam_init):
    j = pl.program_id(2)

    @pl.when(j == 0)
    def _():
        qst_scr[...] = _stack_maps(q_ref[...])
        _attn_init(m_scr, l_scr, acc_scr)
        _online_softmax_step(qst_scr[...], kn_ref[...], vn_ref[...], m_scr, l_scr, acc_scr)

    _online_softmax_step(qst_scr[...], _bf(kc_ref[...]), _bf(vc_ref[...]), m_scr, l_scr, acc_scr)

    @pl.when(j == pl.num_programs(2) - 1)
    def _():
        _attn_finalize(tq, (lq1, lk1, lq2, lk2), sg_ref, o_ref, l_scr, acc_scr, lam_init)


def _attn_sample(q, kb, vb, cache_k, cache_v, lam_vecs, subln_g, nb, seq, lam_init, *, tk=2048):
    past = cache_k.shape[1]
    tk = min(tk, past)
    qspec = pl.BlockSpec((seq, LANES), lambda b, h, j: (b, h))
    cspec = pl.BlockSpec((None, tk, LANES), lambda b, h, j: (b, j, h))
    vec = _const_spec((1, DA_HD))
    return pl.pallas_call(
        functools.partial(_attn_sample_kernel, tq=seq, lam_init=lam_init),
        out_shape=jax.ShapeDtypeStruct((nb * seq, DA_WIDTH), BF16),
        grid=(nb, DA_HEADS, past // tk),
        in_specs=[qspec, qspec, qspec, cspec, cspec, vec, vec, vec, vec, _const_spec((1, LANES))],
        out_specs=qspec,
        scratch_shapes=_attn_scratch(seq),
        compiler_params=_params("parallel", "parallel", "arbitrary"),
        name="attn_sample",
    )(q, kb, vb, cache_k, cache_v, *lam_vecs, subln_g)


def _split3(x):
    hi = _bf(x)
    r1 = x - hi.astype(F32)
    mid = _bf(r1)
    lo = _bf(r1 - mid.astype(F32))
    return hi, mid, lo


def _seg_sum(x, ones_bd):
    hi = _bf(x)
    lo = _bf(x - hi.astype(F32))
    return _dot(hi, ones_bd) + _dot(lo, ones_bd)


def _stack_heads(x):
    lane = lax.broadcasted_iota(jnp.int32, x.shape, 1)
    zero = jnp.zeros_like(x)
    return _bf(jnp.concatenate([jnp.where(lane < RW_HD, x, zero), jnp.where(lane >= RW_HD, x, zero)], axis=0))


def _unit_lower_inverse(a_list, eye, blk16, blk32):
    a16 = [jnp.where(blk16, a, 0.0) for a in a_list]
    a32 = [jnp.where(blk32, a, 0.0) - d for a, d in zip(a_list, a16)]
    a64 = [a - d - e for a, d, e in zip(a_list, a16, a32)]
    t = [eye - d for d in a16]
    q = a16
    for _ in range(3):
        qb = [_bf(x) for x in q]
        q = [_dot(x, x) for x in qb]
        t = [x + _dot(_bf(x), _bf(y)) for x, y in zip(t, q)]
    for off in (a32, a64):
        tb = [_bf(x) for x in t]
        inner = [_bf(_dot(_bf(o), x)) for o, x in zip(off, tb)]
        t = [x - _dot(xb, i) for x, xb, i in zip(t, tb, inner)]
    return t


def _rwkv_kernel(rkv_ref, lora_ref, prkv_ref, plora_ref, s0_ref, mu_rkv_ref, mu_lora_ref, w2_ref,
                 w0_ref, a0_ref, kk_ref, ka_ref, rk_ref, lg_ref, lb_ref,
                 ob_ref, sout_ref, s_scr, prev_rkv_scr, prev_lora_scr):
    c = pl.program_id(1)

    @pl.when(c == 0)
    def _():
        s_scr[...] = s0_ref[...]
        prev_rkv_scr[...] = prkv_ref[...]
        prev_lora_scr[...] = plora_ref[...]

    row = lax.broadcasted_iota(jnp.int32, (CHUNK, 1), 0)

    def token_shift(u, prev, mu):
        u_prev = jnp.where(row == 0, prev, pltpu.roll(u, 1, 0))
        return u + mu * (u_prev - u)

    u = rkv_ref[...]
    ul = lora_ref[...]
    us = token_shift(u, prev_rkv_scr[...], mu_rkv_ref[...])
    usl = token_shift(ul, prev_lora_scr[...], mu_lora_ref[...])
    prev_rkv_scr[...] = u[CHUNK - 1:CHUNK, :]
    prev_lora_scr[...] = ul[CHUNK - 1:CHUNK, :]

    lane_l = lax.broadcasted_iota(jnp.int32, usl.shape, 1)
    act = jnp.where(lane_l < RW_DECAY_LORA, jnp.tanh(usl),
                    jnp.where(lane_l < RW_DECAY_LORA + RW_AAA_LORA, usl, jax.nn.sigmoid(usl)))
    lin = _dot(_bf(act), w2_ref[...])

    r = us[:, 0:RW_WIDTH]
    k = us[:, RW_WIDTH:2 * RW_WIDTH]
    v = us[:, 2 * RW_WIDTH:3 * RW_WIDTH]
    wx = -(w0_ref[...] + lin[:, 0:RW_WIDTH])
    w = -(jnp.maximum(wx, 0.0) + jnp.log(1.0 + jnp.exp(-jnp.abs(wx)))) - 0.5
    log_decay = -jnp.exp(w)
    a = jax.nn.sigmoid(a0_ref[...] + lin[:, RW_WIDTH:2 * RW_WIDTH])
    g = lin[:, 2 * RW_WIDTH:3 * RW_WIDTH]

    ti = lax.broadcasted_iota(jnp.int32, (CHUNK, CHUNK), 0)
    si = lax.broadcasted_iota(jnp.int32, (CHUNK, CHUNK), 1)
    tri = jnp.where(ti >= si, 1.0, 0.0).astype(BF16)
    cs = sum(_dot(tri, part) for part in _split3(log_decay))

    ri = lax.broadcasted_iota(jnp.int32, (PAIR, PAIR), 0)
    ci = lax.broadcasted_iota(jnp.int32, (PAIR, PAIR), 1)
    eye = jnp.where(ri == ci, 1.0, 0.0).astype(F32)
    same_head = (ri // RW_HD) == (ci // RW_HD)
    ones_bd = jnp.where(same_head, 1.0, 0.0).astype(BF16)
    strict = same_head & (ri > ci)
    incl = same_head & (ri >= ci)
    blk16 = (ri // 16) == (ci // 16)
    blk32 = (ri // 32) == (ci // 32)

    pairs = range(RW_HEADS // 2)
    sls = [slice(p * PAIR, (p + 1) * PAIR) for p in pairs]
    cat = lambda xs, ys: [jnp.concatenate([x, y], axis=0) for x, y in zip(xs, ys)]

    kk = [k[:, sl] * kk_ref[:, sl] for sl in sls]
    kk_ss = [_seg_sum(x * x, ones_bd) for x in kk]
    kk = [x * lax.rsqrt(jnp.maximum(ss, 1e-24)) for x, ss in zip(kk, kk_ss)]
    a_p = [a[:, sl] for sl in sls]
    k_p = [k[:, sl] * (1.0 + (ap - 1.0) * ka_ref[:, sl]) for sl, ap in zip(sls, a_p)]
    r_p = [r[:, sl] for sl in sls]
    v_p = [v[:, sl] for sl in sls]
    g_fwd = [jnp.exp(cs[:, sl]) for sl in sls]
    g_inv = [jnp.exp(-cs[:, sl]) for sl in sls]
    g_prev = [jnp.exp(cs[:, sl] - log_decay[:, sl]) for sl in sls]

    ka_s = [_stack_heads(x * gp) for x, gp in zip(kk, g_prev)]
    b_s = [_stack_heads(x * ap * gi) for x, ap, gi in zip(kk, a_p, g_inv)]
    k_s = [_stack_heads(x * gi) for x, gi in zip(k_p, g_inv)]
    r_s = [_stack_heads(x * gf) for x, gf in zip(r_p, g_fwd)]
    v_s = [_stack_heads(x) for x in v_p]
    bk_s = cat(b_s, k_s)

    a_all = [_dot_nt(x, y) for x, y in zip(cat(ka_s, r_s), bk_s)]
    a_ab = [jnp.where(strict, x[:PAIR, :PAIR], 0.0) for x in a_all]
    a_ak = [_bf(jnp.where(strict, x[:PAIR, PAIR:], 0.0)) for x in a_all]
    a_rb = [_bf(jnp.where(incl, x[PAIR:, :PAIR], 0.0)) for x in a_all]
    a_rk = [_bf(jnp.where(incl, x[PAIR:, PAIR:], 0.0)) for x in a_all]
    t_inv = [_bf(t) for t in _unit_lower_inverse(a_ab, eye, blk16, blk32)]

    s_old = [s_scr[p] for p in pairs]
    s_b = [_bf(x) for x in s_old]
    rhs_s = [_dot_nt(x, sb) for x, sb in zip(ka_s, s_b)]
    rhs_v = [_dot(x, vs) for x, vs in zip(a_ak, v_s)]
    sa_b = [_bf(_dot(t, _bf(-(x + y)))) for t, x, y in zip(t_inv, rhs_s, rhs_v)]
    y_s = [_dot_nt(x, sb) for x, sb in zip(r_s, s_b)]
    y_v = [_dot(x, vs) for x, vs in zip(a_rk, v_s)]
    y_sa = [_dot(x, sa) for x, sa in zip(a_rb, sa_b)]
    upd = [_dot_tn(x, y) for x, y in zip(cat(sa_b, v_s), bk_s)]
    for p in pairs:
        s_scr[p] = g_fwd[p][CHUNK - 1:CHUNK, :] * (s_old[p] + upd[p])

    y_bd = [x + y + z for x, y, z in zip(y_s, y_v, y_sa)]
    y = [x[:CHUNK] + x[CHUNK:] for x in y_bd]
    mean = [_seg_sum(x, ones_bd) * (1.0 / RW_HD) for x in y]
    bonus = [_seg_sum(rp * kp * rk_ref[:, sl], ones_bd) * vp for rp, kp, vp, sl in zip(r_p, k_p, v_p, sls)]
    yc = [x - m for x, m in zip(y, mean)]
    var = [_seg_sum(x * x, ones_bd) * (1.0 / RW_HD) for x in yc]
    for p, sl in enumerate(sls):
        yn = yc[p] * lax.rsqrt(var[p] + LNX_EPS) * lg_ref[:, sl] + lb_ref[:, sl]
        ob_ref[:, sl] = ((yn + bonus[p]) * g[:, sl]).astype(ob_ref.dtype)

    @pl.when(c == pl.num_programs(1) - 1)
    def _():
        sout_ref[...] = s_scr[...]


def _rwkv(rkv, lora, prev_rkv, prev_lora, s0_bd, w, nb, seq):
    nc = seq // CHUNK
    npair = RW_HEADS // 2
    tok = lambda width: pl.BlockSpec((CHUNK, width), lambda b, c: (b * nc + c, 0))
    per_seq = lambda width: pl.BlockSpec((None, 1, width), lambda b, c: (b, 0, 0))
    state = pl.BlockSpec((None, npair, PAIR, PAIR), lambda b, c: (b, 0, 0, 0))
    vec = _const_spec((1, RW_WIDTH))
    return pl.pallas_call(
        _rwkv_kernel,
        out_shape=(jax.ShapeDtypeStruct((nb * seq, RW_WIDTH), BF16),
                   jax.ShapeDtypeStruct((nb, npair, PAIR, PAIR), F32)),
        grid=(nb, nc),
        in_specs=[tok(3 * RW_WIDTH), tok(RW_LORA_PAD), per_seq(3 * RW_WIDTH), per_seq(RW_LORA_PAD), state,
                  _const_spec((1, 3 * RW_WIDTH)), _const_spec((1, RW_LORA_PAD)),
                  _const_spec((RW_LORA_PAD, 3 * RW_WIDTH)), vec, vec, vec, vec, vec, vec, vec],
        out_specs=(tok(RW_WIDTH), state),
        scratch_shapes=[pltpu.VMEM((npair, PAIR, PAIR), F32), pltpu.VMEM((1, 3 * RW_WIDTH), F32),
                        pltpu.VMEM((1, RW_LORA_PAD), F32)],
        compiler_params=_params("parallel", "arbitrary"),
        name="rwkv",
    )(rkv, lora, prev_rkv, prev_lora, s0_bd, w["mu_rkv"], w["mu_lora"], w["w2cat"],
      w["w0"], w["a0"], w["k_k"], w["k_a"], w["r_k"], w["lnx_g"], w["lnx_b"])


def _merge_kernel(x_ref, oa_ref, ob_ref, ga_ref, gb_ref, wa_ref, wb_ref, wo_ref, o_ref):
    merged = (ga_ref[...].astype(F32) * _dot(oa_ref[...], wa_ref[...])
              + gb_ref[...].astype(F32) * _dot(ob_ref[...], wb_ref[...]))
    o_ref[...] = x_ref[...] + _dot(_bf(merged), wo_ref[...])


def _merge(x, oa, ob, ga, gb, wa, wb, wo, *, tm=256):
    n = x.shape[0]
    tm = min(tm, n)
    row = lambda w: pl.BlockSpec((tm, w), lambda i: (i, 0))
    return pl.pallas_call(
        _merge_kernel,
        out_shape=jax.ShapeDtypeStruct((n, D_MODEL), F32),
        grid=(n // tm,),
        in_specs=[row(D_MODEL), row(DA_WIDTH), row(RW_WIDTH), row(D_MODEL), row(D_MODEL),
                  _const_spec((DA_WIDTH, D_MODEL)), _const_spec((RW_WIDTH, D_MODEL)),
                  _const_spec((D_MODEL, D_MODEL))],
        out_specs=row(D_MODEL),
        compiler_params=_params("parallel"),
        name="merge",
    )(x, oa, ob, ga, gb, wa, wb, wo)


def _prep_weights(l, p):
    row = lambda v: v.reshape(1, -1).astype(F32)
    w_in = p["w_in"][l]
    o_rw = 3 * DA_WIDTH
    o_g = o_rw + RW_PROJ
    w_lora = jnp.pad(w_in[:, o_rw + 3 * RW_WIDTH:o_g], ((0, 0), (0, RW_LORA_PAD - RW_LORA)))
    w2cat = jnp.zeros((RW_LORA_PAD, 3 * RW_WIDTH), F32)
    w2cat = w2cat.at[0:RW_DECAY_LORA, 0:RW_WIDTH].set(p["rw_w2"][l])
    w2cat = w2cat.at[RW_DECAY_LORA:RW_DECAY_LORA + RW_AAA_LORA, RW_WIDTH:2 * RW_WIDTH].set(p["rw_a2"][l])
    w2cat = w2cat.at[RW_DECAY_LORA + RW_AAA_LORA:RW_LORA, 2 * RW_WIDTH:].set(p["rw_g2"][l])
    mu = p["shift_mu"][l]
    return dict(
        ffn1_norm=row(p["ffn1_norm"][l]), ffn1_wi=_bf(p["ffn1_wi"][l]), ffn1_wo=_bf(p["ffn1_wo"][l]),
        ffn2_norm=row(p["ffn2_norm"][l]), ffn2_wi=_bf(p["ffn2_wi"][l]), ffn2_wo=_bf(p["ffn2_wo"][l]),
        mix_norm=row(p["mix_norm"][l]),
        wq=_bf(w_in[:, 0:DA_WIDTH]), wk=_bf(w_in[:, DA_WIDTH:2 * DA_WIDTH]), wv=_bf(w_in[:, 2 * DA_WIDTH:o_rw]),
        wq_t=_bf(w_in[:, 0:DA_WIDTH].T), wv_t=_bf(w_in[:, 2 * DA_WIDTH:o_rw].T),
        w_rkv=_bf(w_in[:, o_rw:o_rw + 3 * RW_WIDTH]), w_lora=_bf(w_lora),
        w_ga=_bf(w_in[:, o_g:o_g + D_MODEL]), w_gb=_bf(w_in[:, o_g + D_MODEL:]),
        lam_vecs=tuple(row(p[n][l]) for n in ("lambda_q1", "lambda_k1", "lambda_q2", "lambda_k2")),
        subln_g=row(p["subln_g"][l]),
        mu_rkv=row(mu[:3 * RW_WIDTH]), mu_lora=row(jnp.pad(mu[3 * RW_WIDTH:], (0, RW_LORA_PAD - RW_LORA))),
        w2cat=_bf(w2cat), w0=row(p["rw_w0"][l]), a0=row(p["rw_a0"][l]), k_k=row(p["rw_k_k"][l]),
        k_a=row(p["rw_k_a"][l]), r_k=row(p["rw_r_k"][l]), lnx_g=row(p["rw_lnx_g"][l]), lnx_b=row(p["rw_lnx_b"][l]),
        w_proj_a=_bf(p["w_proj_a"][l]), w_proj_b=_bf(p["w_proj_b"][l]), w_out=_bf(p["w_out"][l]),
    )


def _state_to_pairs(s):
    nb = s.shape[0]
    s = s.reshape(nb, RW_HEADS // 2, 2, RW_HD, 1, RW_HD)
    sel = jnp.eye(2, dtype=s.dtype).reshape(1, 1, 2, 1, 2, 1)
    return (s * sel).reshape(nb, RW_HEADS // 2, PAIR, PAIR)


def _pairs_to_state(s):
    nb = s.shape[0]
    s = s.reshape(nb, RW_HEADS // 2, 2, RW_HD, 2, RW_HD)
    return jnp.stack([s[:, :, 0, :, 0, :], s[:, :, 1, :, 1, :]], axis=2).reshape(nb, RW_HEADS, RW_HD, RW_HD)


def _layer(x, l, nb, seq, w, cache_k, cache_v, s0, prev, final_g):
    lam_init = 0.8 - 0.6 * math.exp(-0.3 * l)
    x = _ffn(x, w["ffn1_norm"], w["ffn1_wi"], w["ffn1_wo"])
    prompt = cache_k is None
    tq = min(ATTN_TQ, seq)
    q, k, v, kb, vb = _qkv_proj(x, w["mix_norm"], w["wq_t"] if prompt else w["wq"], w["wk"], w["wv"], w["wv_t"],
                                transposed=prompt, tm=tq if prompt else min(256, nb * seq))
    rkv, lora = _rw_proj(x, w["mix_norm"], w["w_rkv"], w["w_lora"])
    ga, gb = _gate_proj(x, w["mix_norm"], w["w_ga"], w["w_gb"])
    if prompt:
        o_a = _attn_prompt(q, kb, vb, w["lam_vecs"], w["subln_g"], nb, seq, lam_init, tq=tq)
        s0 = jnp.zeros((nb, RW_HEADS, RW_HD, RW_HD), F32)
        prev = jnp.zeros((nb, 1, RW_PROJ), F32)
    else:
        past = cache_k.shape[1]
        o_a = _attn_sample(q, kb, vb, cache_k.reshape(nb, past, DA_WIDTH), cache_v.reshape(nb, past, DA_WIDTH),
                           w["lam_vecs"], w["subln_g"], nb, seq, lam_init)
    prev_rkv = prev[:, :, :3 * RW_WIDTH]
    prev_lora = jnp.pad(prev[:, :, 3 * RW_WIDTH:], ((0, 0), (0, 0), (0, RW_LORA_PAD - RW_LORA)))
    o_b, s_new = _rwkv(rkv, lora, prev_rkv, prev_lora, _state_to_pairs(s0), w, nb, seq)
    x = _merge(x, o_a, o_b, ga, gb, w["w_proj_a"], w["w_proj_b"], w["w_out"])
    x = _ffn(x, w["ffn2_norm"], w["ffn2_wi"], w["ffn2_wo"], final_g)
    last = jnp.concatenate([rkv.reshape(nb, seq, -1)[:, -1:], lora.reshape(nb, seq, -1)[:, -1:, :RW_LORA]], axis=-1)
    return (x, k.reshape(nb, seq, DA_HEADS, 2 * DA_HD), v.reshape(nb, seq, DA_HEADS, 2 * DA_HD),
            _pairs_to_state(s_new), last)


def _run_stream(x, weights, final_norm, caches):
    nb, seq, _ = x.shape
    x = x.reshape(nb * seq, D_MODEL)
    ks, vs, ss, shs = [], [], [], []
    depth = len(weights)
    for l, w in enumerate(weights):
        final_g = final_norm.reshape(1, -1) if l == depth - 1 else None
        if caches is None:
            ck = cv = s0 = prev = None
        else:
            ck, cv, s0, prev = (c[l] for c in caches)
        x, k, v, s, sh = _layer(x, l, nb, seq, w, ck, cv, s0, prev, final_g)
        ks.append(k); vs.append(v); ss.append(s); shs.append(sh)
    return x.reshape(nb, seq, D_MODEL), jnp.stack(ks), jnp.stack(vs), jnp.stack(ss), jnp.stack(shs)


def kernel(x_prompt, x_sample, cache_k, cache_v, state_rwkv, state_shift, ffn1_norm, ffn1_wi, ffn1_wo, mix_norm, w_in, lambda_q1, lambda_k1, lambda_q2, lambda_k2, subln_g, shift_mu, rw_w0, rw_w2, rw_a0, rw_a2, rw_g2, rw_k_k, rw_k_a, rw_r_k, rw_lnx_g, rw_lnx_b, w_proj_a, w_proj_b, w_out, ffn2_norm, ffn2_wi, ffn2_wo, final_norm):
    params = dict(ffn1_norm=ffn1_norm, ffn1_wi=ffn1_wi, ffn1_wo=ffn1_wo, mix_norm=mix_norm, w_in=w_in,
                  lambda_q1=lambda_q1, lambda_k1=lambda_k1, lambda_q2=lambda_q2, lambda_k2=lambda_k2,
                  subln_g=subln_g, shift_mu=shift_mu, rw_w0=rw_w0, rw_w2=rw_w2, rw_a0=rw_a0, rw_a2=rw_a2,
                  rw_g2=rw_g2, rw_k_k=rw_k_k, rw_k_a=rw_k_a, rw_r_k=rw_r_k, rw_lnx_g=rw_lnx_g, rw_lnx_b=rw_lnx_b,
                  w_proj_a=w_proj_a, w_proj_b=w_proj_b, w_out=w_out, ffn2_norm=ffn2_norm, ffn2_wi=ffn2_wi,
                  ffn2_wo=ffn2_wo)
    weights = [_prep_weights(l, params) for l in range(ffn1_wi.shape[0])]
    y_p, k_p, v_p, s_p, sh_p = _run_stream(x_prompt, weights, final_norm, None)
    y_s, k_s, v_s, s_s, sh_s = _run_stream(x_sample, weights, final_norm,
                                           (cache_k, cache_v, state_rwkv, state_shift))
    return (y_p, y_s, k_p, v_p, s_p, sh_p, k_s, v_s, s_s, sh_s)
```

```python
import functools
import math

import jax
import jax.numpy as jnp
from jax import lax
from jax.experimental import pallas as pl
from jax.experimental.pallas import tpu as pltpu

F32 = jnp.float32
BF16 = jnp.bfloat16

D_MODEL = 2048
DA_HEADS = 8
DA_HD = 64
DA_WIDTH = DA_HEADS * 2 * DA_HD
RW_HEADS = 16
RW_HD = 64
RW_WIDTH = RW_HEADS * RW_HD
RW_DECAY_LORA = 96
RW_AAA_LORA = 96
RW_GATE_LORA = 256
RW_LORA = RW_DECAY_LORA + RW_AAA_LORA + RW_GATE_LORA
RW_LORA_PAD = 512
RW_PROJ = 3 * RW_WIDTH + RW_LORA
FFN_DIM = 11 * D_MODEL // 4
CHUNK = 64
NORM_EPS = 1e-6
SUBLN_EPS = 1e-5
LNX_EPS = 64e-5
NEG_BIG = -1e30

LANES = 128
BF16_ROWS = 16
PAIR = 2 * RW_HD
VMEM_LIMIT = 56 * 1024 * 1024

NT_DIMS = (((1,), (1,)), ((), ()))
TN_DIMS = (((0,), (0,)), ((), ()))


def _dot(a, b):
    return jnp.dot(a, b, preferred_element_type=F32)


def _dot_nt(a, b):
    return lax.dot_general(a, b, NT_DIMS, preferred_element_type=F32)


def _dot_tn(a, b):
    return lax.dot_general(a, b, TN_DIMS, preferred_element_type=F32)


def _bf(x):
    return x.astype(BF16)


def _rms(x, g, eps):
    return x * lax.rsqrt(jnp.mean(x * x, axis=-1, keepdims=True) + eps) * g


def _params(*sem):
    return pltpu.CompilerParams(dimension_semantics=sem, vmem_limit_bytes=VMEM_LIMIT)


def _const_spec(shape):
    nd = len(shape)
    return pl.BlockSpec(shape, lambda *_: (0,) * nd, pipeline_mode=pl.Buffered(1))


def _ffn_kernel(*refs, final):
    if final:
        x_ref, g_ref, wg_ref, wu_ref, wo_ref, fg_ref, o_ref, h_scr = refs
    else:
        x_ref, g_ref, wg_ref, wu_ref, wo_ref, o_ref, h_scr = refs
    f = pl.program_id(1)

    @pl.when(f == 0)
    def _():
        h_scr[...] = _bf(_rms(x_ref[...], g_ref[...], NORM_EPS))
        o_ref[...] = jnp.zeros_like(o_ref)

    h = h_scr[...]
    gate = _dot(h, wg_ref[...])
    up = _dot(h, wu_ref[...])
    act = _bf(gate * jax.nn.sigmoid(gate) * up)
    o_ref[...] += _dot(act, wo_ref[...])

    @pl.when(f == pl.num_programs(1) - 1)
    def _():
        y = x_ref[...] + 0.5 * o_ref[...]
        if final:
            y = _rms(y, fg_ref[...], NORM_EPS)
        o_ref[...] = y


def _ffn(x, g, wi, wo, final_g=None, *, tm=1024, tf=512):
    n = x.shape[0]
    tm = min(tm, n)
    nf = FFN_DIM // tf
    final = final_g is not None
    in_specs = [
        pl.BlockSpec((tm, D_MODEL), lambda i, f: (i, 0), pipeline_mode=pl.Buffered(1)),
        _const_spec((1, D_MODEL)),
        pl.BlockSpec((D_MODEL, tf), lambda i, f: (0, f)),
        pl.BlockSpec((D_MODEL, tf), lambda i, f: (0, f + nf)),
        pl.BlockSpec((tf, D_MODEL), lambda i, f: (f, 0)),
    ]
    args = [x, g, wi, wi, wo]
    if final:
        in_specs.append(_const_spec((1, D_MODEL)))
        args.append(final_g)
    return pl.pallas_call(
        functools.partial(_ffn_kernel, final=final),
        out_shape=jax.ShapeDtypeStruct((n, D_MODEL), F32),
        grid=(n // tm, nf),
        in_specs=in_specs,
        out_specs=pl.BlockSpec((tm, D_MODEL), lambda i, f: (i, 0)),
        scratch_shapes=[pltpu.VMEM((tm, D_MODEL), BF16)],
        compiler_params=_params("parallel", "arbitrary"),
        name="ffn_final" if final else "ffn",
    )(*args)


def _qkv_kernel(x_ref, g_ref, wq_ref, wk_ref, wv_ref, wvt_ref, q_ref, k_ref, v_ref, kb_ref, vb_ref, *, transposed):
    h = _bf(_rms(x_ref[...], g_ref[...], NORM_EPS))
    q = _dot_nt(wq_ref[...], h) if transposed else _dot(h, wq_ref[...])
    q_ref[...] = _bf(q * (DA_HD ** -0.5))
    k = _dot(h, wk_ref[...])
    k_ref[...] = k
    kb_ref[...] = _bf(k)
    v = _dot(h, wv_ref[...])
    v_ref[...] = v
    vb_ref[...] = _bf(_dot_nt(wvt_ref[...], h)) if transposed else _bf(v)


def _qkv_proj(x, g, wq, wk, wv, wvt, *, transposed, tm):
    n = x.shape[0]
    row = lambda w: pl.BlockSpec((tm, w), lambda i: (i, 0))
    rows = lambda w, dt: jax.ShapeDtypeStruct((n, w), dt)
    wspec = _const_spec(wk.shape)
    if transposed:
        slab = pl.BlockSpec((None, DA_WIDTH, tm), lambda i: (i, 0, 0))
        slabs = jax.ShapeDtypeStruct((n // tm, DA_WIDTH, tm), BF16)
        q_spec, q_shape, vb_spec, vb_shape = slab, slabs, slab, slabs
    else:
        q_spec, q_shape, vb_spec, vb_shape = row(DA_WIDTH), rows(DA_WIDTH, BF16), row(DA_WIDTH), rows(DA_WIDTH, BF16)
    return pl.pallas_call(
        functools.partial(_qkv_kernel, transposed=transposed),
        out_shape=(q_shape, rows(DA_WIDTH, F32), rows(DA_WIDTH, F32), rows(DA_WIDTH, BF16), vb_shape),
        grid=(n // tm,),
        in_specs=[row(D_MODEL), _const_spec((1, D_MODEL)), _const_spec(wq.shape), wspec, wspec,
                  _const_spec(wvt.shape)],
        out_specs=(q_spec, row(DA_WIDTH), row(DA_WIDTH), row(DA_WIDTH), vb_spec),
        compiler_params=_params("parallel"),
        name="qkv_proj_t" if transposed else "qkv_proj",
    )(x, g, wq, wk, wv, wvt)


def _rw_proj_kernel(x_ref, g_ref, wrkv_ref, wl_ref, rkv_ref, lora_ref):
    h = _bf(_rms(x_ref[...], g_ref[...], NORM_EPS))
    rkv_ref[...] = _dot(h, wrkv_ref[...])
    lora_ref[...] = _dot(h, wl_ref[...])


def _rw_proj(x, g, wrkv, wl, *, tm=256):
    n = x.shape[0]
    tm = min(tm, n)
    row = lambda w: pl.BlockSpec((tm, w), lambda i: (i, 0))
    return pl.pallas_call(
        _rw_proj_kernel,
        out_shape=(jax.ShapeDtypeStruct((n, 3 * RW_WIDTH), F32),
                   jax.ShapeDtypeStruct((n, RW_LORA_PAD), F32)),
        grid=(n // tm,),
        in_specs=[row(D_MODEL), _const_spec((1, D_MODEL)),
                  _const_spec((D_MODEL, 3 * RW_WIDTH)), _const_spec((D_MODEL, RW_LORA_PAD))],
        out_specs=(row(3 * RW_WIDTH), row(RW_LORA_PAD)),
        compiler_params=_params("parallel"),
        name="rw_proj",
    )(x, g, wrkv, wl)


def _gate_kernel(x_ref, g_ref, wa_ref, wb_ref, ga_ref, gb_ref):
    h = _bf(_rms(x_ref[...], g_ref[...], NORM_EPS))
    ga_ref[...] = _bf(jax.nn.sigmoid(_dot(h, wa_ref[...])))
    gb_ref[...] = _bf(jax.nn.sigmoid(_dot(h, wb_ref[...])))


def _gate_proj(x, g, wa, wb, *, tm=256):
    n = x.shape[0]
    tm = min(tm, n)
    row = pl.BlockSpec((tm, D_MODEL), lambda i: (i, 0))
    wspec = _const_spec((D_MODEL, D_MODEL))
    return pl.pallas_call(
        _gate_kernel,
        out_shape=(jax.ShapeDtypeStruct((n, D_MODEL), BF16),) * 2,
        grid=(n // tm,),
        in_specs=[row, _const_spec((1, D_MODEL)), wspec, wspec],
        out_specs=(row, row),
        compiler_params=_params("parallel"),
        name="gate_proj",
    )(x, g, wa, wb)


def _stack_maps(q):
    lane = lax.broadcasted_iota(jnp.int32, q.shape, 1)
    zero = jnp.zeros_like(q)
    return jnp.concatenate([jnp.where(lane < DA_HD, q, zero), jnp.where(lane >= DA_HD, q, zero)], axis=0)


def _online_softmax_step(qst, kb, vb, m_scr, l_scr, acc_scr, mask=None):
    s = _dot_nt(qst, kb)
    if mask is not None:
        s = jnp.where(mask, s, NEG_BIG)
    m_prev = m_scr[...]
    m_next = jnp.maximum(m_prev, jnp.max(s, axis=1, keepdims=True))
    alpha = jnp.exp(m_prev - m_next)
    p = jnp.exp(s - m_next)
    l_scr[...] = alpha * l_scr[...] + jnp.sum(p, axis=1, keepdims=True)
    acc_scr[...] = alpha * acc_scr[...] + _dot(_bf(p), vb)
    m_scr[...] = m_next


def _attn_init(m_scr, l_scr, acc_scr):
    m_scr[...] = jnp.full_like(m_scr, NEG_BIG)
    l_scr[...] = jnp.zeros_like(l_scr)
    acc_scr[...] = jnp.zeros_like(acc_scr)


def _attn_finalize(tq, lam_refs, sg_ref, o_ref, l_scr, acc_scr, lam_init):
    lq1, lk1, lq2, lk2 = lam_refs
    lam = (jnp.exp(jnp.sum(lq1[...] * lk1[...], axis=-1, keepdims=True))
           - jnp.exp(jnp.sum(lq2[...] * lk2[...], axis=-1, keepdims=True)) + lam_init)
    o = acc_scr[...] / l_scr[...]
    o = o[:tq] - lam * o[tq:]
    o = o * lax.rsqrt(jnp.mean(o * o, axis=-1, keepdims=True) + SUBLN_EPS)
    o_ref[...] = (o * sg_ref[...] * (1.0 - lam_init)).astype(o_ref.dtype)


ATTN_STRIP = 256
ATTN_TQ = 512


def _attn_prompt_kernel(qt_ref, k_ref, vt_ref, lq1, lk1, lq2, lk2, sg_ref, o_ref,
                        qst_scr, m_scr, acc_scr, s_scr, p_scr, alpha_scr, *, tq, lam_init):
    qi = pl.program_id(2)
    n_strips = 2 * tq // ATTN_STRIP
    last = n_strips - 1
    strip = lambda c: slice(c * ATTN_STRIP, (c + 1) * ATTN_STRIP)
    chunks = tq // CHUNK

    qt = qt_ref[...]
    feat = lax.broadcasted_iota(jnp.int32, qt.shape, 0)
    zero = jnp.zeros_like(qt)
    qst_scr[:, 0:tq] = jnp.where(feat < DA_HD, qt, zero)
    qst_scr[:, tq:2 * tq] = jnp.where(feat >= DA_HD, qt, zero)
    m_scr[...] = jnp.full_like(m_scr, NEG_BIG)
    acc_scr[...] = jnp.zeros_like(acc_scr)
    p_scr[...] = jnp.zeros_like(p_scr)
    alpha_scr[...] = jnp.ones_like(alpha_scr)

    def key_block(j):
        return k_ref[pl.ds(pl.multiple_of(j * tq, tq), tq), :]

    def value_block(j):
        return jnp.concatenate([vt_ref[j], jnp.ones((BF16_ROWS, tq), BF16)], axis=0)

    def chunk_gap(c):
        kc = lax.broadcasted_iota(jnp.int32, (tq, ATTN_STRIP), 0) // CHUNK
        qc = ((lax.broadcasted_iota(jnp.int32, (tq, ATTN_STRIP), 1) + c * ATTN_STRIP) % tq) // CHUNK
        return kc - qc

    def scores(kb, c, blocks_below_diag=None):
        s = _dot(kb, qst_scr[:, strip(c)])
        if blocks_below_diag is not None:
            s = jnp.where(chunk_gap(c) <= blocks_below_diag * chunks, s, NEG_BIG)
        return s

    def softmax(c, s):
        cs = strip(c)
        m_prev = m_scr[:, cs]
        m_next = jnp.maximum(m_prev, jnp.max(s, axis=0, keepdims=True))
        m_scr[:, cs] = m_next
        return _bf(jnp.exp(s - m_next)), jnp.exp(m_prev - m_next)

    def accumulate(c, vt, p, alpha):
        cs = strip(c)
        acc_scr[:, cs] = alpha * acc_scr[:, cs] + _dot(vt, p)

    def run_block(j, vt_prev, diag):
        kb = key_block(j)
        vt = value_block(j)
        mask = 0 if diag else None
        s_next = scores(kb, 1, mask)
        p, alpha = softmax(0, s_scr[...])
        accumulate(last, vt_prev, p_scr[...], alpha_scr[...])
        for c in range(1, n_strips):
            s = s_next
            if c < last:
                s_next = scores(kb, c + 1, mask)
            elif not diag:
                s_scr[...] = scores(key_block(j + 1), 0, qi - (j + 1))
            accumulate(c - 1, vt, p, alpha)
            p, alpha = softmax(c, s)
        return vt, p, alpha

    s_scr[...] = scores(key_block(0), 0, qi)

    def body(j, carry):
        vt_prev = value_block(jnp.maximum(j - 1, 0))
        _, p, alpha = run_block(j, vt_prev, False)
        p_scr[...] = p
        alpha_scr[...] = alpha
        return carry

    lax.fori_loop(0, qi, body, 0)
    vt, p, alpha = run_block(qi, value_block(jnp.maximum(qi - 1, 0)), True)
    accumulate(last, vt, p, alpha)

    lam = (jnp.exp(jnp.sum(lq1[...] * lk1[...], axis=-1, keepdims=True))
           - jnp.exp(jnp.sum(lq2[...] * lk2[...], axis=-1, keepdims=True)) + lam_init)
    o = acc_scr[0:LANES, :] / acc_scr[LANES:LANES + 1, :]
    o = o[:, :tq] - lam * o[:, tq:]
    o = o * lax.rsqrt(jnp.mean(o * o, axis=0, keepdims=True) + SUBLN_EPS)
    o = o * sg_ref[...] * (1.0 - lam_init)
    o_ref[...] = o.T.astype(o_ref.dtype)


def _attn_prompt(qt, kb, vt, lam_vecs, subln_g, nb, seq, lam_init, *, tq):
    nq = seq // tq
    vec = _const_spec((1, DA_HD))
    return pl.pallas_call(
        functools.partial(_attn_prompt_kernel, tq=tq, lam_init=lam_init),
        out_shape=jax.ShapeDtypeStruct((nb * seq, DA_WIDTH), BF16),
        grid=(nb, DA_HEADS, nq),
        in_specs=[pl.BlockSpec((None, LANES, tq), lambda b, h, i: (b * nq + i, h, 0)),
                  pl.BlockSpec((seq, LANES), lambda b, h, i: (b, h)),
                  pl.BlockSpec((nq, LANES, tq), lambda b, h, i: (b, h, 0)),
                  vec, vec, vec, vec, _const_spec((LANES, 1))],
        out_specs=pl.BlockSpec((tq, LANES), lambda b, h, i: (b * nq + i, h)),
        scratch_shapes=[pltpu.VMEM((LANES, 2 * tq), BF16), pltpu.VMEM((1, 2 * tq), F32),
                        pltpu.VMEM((LANES + BF16_ROWS, 2 * tq), F32), pltpu.VMEM((tq, ATTN_STRIP), F32),
                        pltpu.VMEM((tq, ATTN_STRIP), BF16), pltpu.VMEM((1, ATTN_STRIP), F32)],
        compiler_params=_params("parallel", "parallel", "arbitrary"),
        name="attn_prompt",
    )(qt, kb, vt, *lam_vecs, subln_g.reshape(LANES, 1))


def _attn_sample_kernel(q_ref, kn_ref, vn_ref, kc_ref, vc_ref, lq1, lk1, lq2, lk2, sg_ref, o_ref,
                        qst_scr, m_scr, l_scr, acc_scr, *, tq, lam_init):
    j = pl.program_id(1)
    heads = [slice(h * LANES, (h + 1) * LANES) for h in range(DA_HEADS)]
    state = lambda h: (m_scr.at[h], l_scr.at[h], acc_scr.at[h])

    @pl.when(j == 0)
    def _():
        for h, cols in enumerate(heads):
            qst_scr[h] = _stack_maps(q_ref[:, cols])
            _attn_init(*state(h))
            _online_softmax_step(qst_scr[h], kn_ref[:, cols], vn_ref[:, cols], *state(h))

    for h, cols in enumerate(heads):
        _online_softmax_step(qst_scr[h], kc_ref[:, cols], vc_ref[:, cols], *state(h))

    @pl.when(j == pl.num_programs(1) - 1)
    def _():
        for h, cols in enumerate(heads):
            _attn_finalize(tq, (lq1, lk1, lq2, lk2), sg_ref, o_ref.at[:, cols], l_scr.at[h], acc_scr.at[h], lam_init)


def _attn_sample(q, kb, vb, cache_k, cache_v, lam_vecs, subln_g, nb, seq, lam_init, *, tk=2048):
    past = cache_k.shape[1]
    tk = min(tk, past)
    qspec = pl.BlockSpec((seq, DA_WIDTH), lambda b, j: (b, 0))
    cspec = pl.BlockSpec((None, tk, DA_WIDTH), lambda b, j: (b, j, 0))
    vec = _const_spec((1, DA_HD))
    return pl.pallas_call(
        functools.partial(_attn_sample_kernel, tq=seq, lam_init=lam_init),
        out_shape=jax.ShapeDtypeStruct((nb * seq, DA_WIDTH), BF16),
        grid=(nb, past // tk),
        in_specs=[qspec, qspec, qspec, cspec, cspec, vec, vec, vec, vec, _const_spec((1, LANES))],
        out_specs=qspec,
        scratch_shapes=[pltpu.VMEM((DA_HEADS, 2 * seq, LANES), BF16), pltpu.VMEM((DA_HEADS, 2 * seq, 1), F32),
                        pltpu.VMEM((DA_HEADS, 2 * seq, 1), F32), pltpu.VMEM((DA_HEADS, 2 * seq, LANES), F32)],
        compiler_params=_params("parallel", "arbitrary"),
        name="attn_sample",
    )(q, kb, vb, cache_k, cache_v, *lam_vecs, subln_g)


def _split3(x):
    hi = _bf(x)
    r1 = x - hi.astype(F32)
    mid = _bf(r1)
    lo = _bf(r1 - mid.astype(F32))
    return hi, mid, lo


def _seg_sum(x, ones_bd):
    hi = _bf(x)
    lo = _bf(x - hi.astype(F32))
    return _dot(hi, ones_bd) + _dot(lo, ones_bd)


def _stack_heads(x):
    lane = lax.broadcasted_iota(jnp.int32, x.shape, 1)
    zero = jnp.zeros_like(x)
    return _bf(jnp.concatenate([jnp.where(lane < RW_HD, x, zero), jnp.where(lane >= RW_HD, x, zero)], axis=0))


def _unit_lower_inverse(a_list, eye, blk16, blk32):
    a16 = [jnp.where(blk16, a, 0.0) for a in a_list]
    a32 = [jnp.where(blk32, a, 0.0) - d for a, d in zip(a_list, a16)]
    a64 = [a - d - e for a, d, e in zip(a_list, a16, a32)]
    t = [eye - d for d in a16]
    q = a16
    for _ in range(3):
        qb = [_bf(x) for x in q]
        q = [_dot(x, x) for x in qb]
        t = [x + _dot(_bf(x), _bf(y)) for x, y in zip(t, q)]
    for off in (a32, a64):
        tb = [_bf(x) for x in t]
        inner = [_bf(_dot(_bf(o), x)) for o, x in zip(off, tb)]
        t = [x - _dot(xb, i) for x, xb, i in zip(t, tb, inner)]
    return t


def _rwkv_kernel(rkv_ref, lora_ref, prkv_ref, plora_ref, s0_ref, mu_rkv_ref, mu_lora_ref, w2_ref,
                 w0_ref, a0_ref, kk_ref, ka_ref, rk_ref, lg_ref, lb_ref,
                 ob_ref, sout_ref, s_scr, prev_rkv_scr, prev_lora_scr, *, group):
    c = pl.program_id(1)

    @pl.when(c == 0)
    def _():
        s_scr[...] = s0_ref[...]
        prev_rkv_scr[...] = prkv_ref[...]
        prev_lora_scr[...] = plora_ref[...]

    row = lax.broadcasted_iota(jnp.int32, (CHUNK, 1), 0)
    ti = lax.broadcasted_iota(jnp.int32, (CHUNK, CHUNK), 0)
    si = lax.broadcasted_iota(jnp.int32, (CHUNK, CHUNK), 1)
    tri = jnp.where(ti >= si, 1.0, 0.0).astype(BF16)

    def token_shift(u, prev, mu):
        u_prev = jnp.where(row == 0, prev, pltpu.roll(u, 1, 0))
        return u + mu * (u_prev - u)

    def per_token(s):
        u = rkv_ref[s]
        ul = lora_ref[s]
        us = token_shift(u, prev_rkv_scr[s], mu_rkv_ref[...])
        usl = token_shift(ul, prev_lora_scr[s], mu_lora_ref[...])
        prev_rkv_scr[s] = u[CHUNK - 1:CHUNK, :]
        prev_lora_scr[s] = ul[CHUNK - 1:CHUNK, :]
        lane_l = lax.broadcasted_iota(jnp.int32, usl.shape, 1)
        act = jnp.where(lane_l < RW_DECAY_LORA, jnp.tanh(usl),
                        jnp.where(lane_l < RW_DECAY_LORA + RW_AAA_LORA, usl, jax.nn.sigmoid(usl)))
        lin = _dot(_bf(act), w2_ref[...])
        wx = -(w0_ref[...] + lin[:, 0:RW_WIDTH])
        w = -(jnp.maximum(wx, 0.0) + jnp.log(1.0 + jnp.exp(-jnp.abs(wx)))) - 0.5
        log_decay = -jnp.exp(w)
        cs = sum(_dot(tri, part) for part in _split3(log_decay))
        return dict(r=us[:, 0:RW_WIDTH], k=us[:, RW_WIDTH:2 * RW_WIDTH], v=us[:, 2 * RW_WIDTH:3 * RW_WIDTH],
                    a=jax.nn.sigmoid(a0_ref[...] + lin[:, RW_WIDTH:2 * RW_WIDTH]),
                    g=lin[:, 2 * RW_WIDTH:3 * RW_WIDTH], cs=cs, log_decay=log_decay)

    tok = [per_token(s) for s in range(group)]

    ri = lax.broadcasted_iota(jnp.int32, (PAIR, PAIR), 0)
    ci = lax.broadcasted_iota(jnp.int32, (PAIR, PAIR), 1)
    eye = jnp.where(ri == ci, 1.0, 0.0).astype(F32)
    same_head = (ri // RW_HD) == (ci // RW_HD)
    ones_bd = jnp.where(same_head, 1.0, 0.0).astype(BF16)
    strict = same_head & (ri > ci)
    incl = same_head & (ri >= ci)
    blk16 = (ri // 16) == (ci // 16)
    blk32 = (ri // 32) == (ci // 32)

    units = [(s, p) for s in range(group) for p in range(RW_HEADS // 2)]
    sls = [slice(p * PAIR, (p + 1) * PAIR) for _, p in units]
    col = lambda name: [tok[s][name][:, sl] for (s, _), sl in zip(units, sls)]
    cat = lambda xs, ys: [jnp.concatenate([x, y], axis=0) for x, y in zip(xs, ys)]

    k_raw, a_p, r_p, v_p, cs, log_decay = col("k"), col("a"), col("r"), col("v"), col("cs"), col("log_decay")
    kk = [x * kk_ref[:, sl] for x, sl in zip(k_raw, sls)]
    kk_ss = [_seg_sum(x * x, ones_bd) for x in kk]
    kk = [x * lax.rsqrt(jnp.maximum(ss, 1e-24)) for x, ss in zip(kk, kk_ss)]
    k_p = [x * (1.0 + (ap - 1.0) * ka_ref[:, sl]) for x, ap, sl in zip(k_raw, a_p, sls)]
    g_fwd = [jnp.exp(x) for x in cs]
    g_inv = [jnp.exp(-x) for x in cs]
    g_prev = [jnp.exp(x - ld) for x, ld in zip(cs, log_decay)]

    ka_s = [_stack_heads(x * gp) for x, gp in zip(kk, g_prev)]
    b_s = [_stack_heads(x * ap * gi) for x, ap, gi in zip(kk, a_p, g_inv)]
    k_s = [_stack_heads(x * gi) for x, gi in zip(k_p, g_inv)]
    r_s = [_stack_heads(x * gf) for x, gf in zip(r_p, g_fwd)]
    v_s = [_stack_heads(x) for x in v_p]
    bk_s = cat(b_s, k_s)

    a_all = [_dot_nt(x, y) for x, y in zip(cat(ka_s, r_s), bk_s)]
    a_ab = [jnp.where(strict, x[:PAIR, :PAIR], 0.0) for x in a_all]
    a_ak = [_bf(jnp.where(strict, x[:PAIR, PAIR:], 0.0)) for x in a_all]
    a_rb = [_bf(jnp.where(incl, x[PAIR:, :PAIR], 0.0)) for x in a_all]
    a_rk = [_bf(jnp.where(incl, x[PAIR:, PAIR:], 0.0)) for x in a_all]
    t_inv = [_bf(t) for t in _unit_lower_inverse(a_ab, eye, blk16, blk32)]

    s_old = [s_scr[s, p] for s, p in units]
    s_b = [_bf(x) for x in s_old]
    rhs_s = [_dot_nt(x, sb) for x, sb in zip(ka_s, s_b)]
    rhs_v = [_dot(x, vs) for x, vs in zip(a_ak, v_s)]
    sa_b = [_bf(_dot(t, _bf(-(x + y)))) for t, x, y in zip(t_inv, rhs_s, rhs_v)]
    y_s = [_dot_nt(x, sb) for x, sb in zip(r_s, s_b)]
    y_v = [_dot(x, vs) for x, vs in zip(a_rk, v_s)]
    y_sa = [_dot(x, sa) for x, sa in zip(a_rb, sa_b)]
    upd = [_dot_tn(x, y) for x, y in zip(cat(sa_b, v_s), bk_s)]
    for i, (s, p) in enumerate(units):
        s_scr[s, p] = g_fwd[i][CHUNK - 1:CHUNK, :] * (s_old[i] + upd[i])

    y_bd = [x + y + z for x, y, z in zip(y_s, y_v, y_sa)]
    y = [x[:CHUNK] + x[CHUNK:] for x in y_bd]
    mean = [_seg_sum(x, ones_bd) * (1.0 / RW_HD) for x in y]
    bonus = [_seg_sum(rp * kp * rk_ref[:, sl], ones_bd) * vp for rp, kp, vp, sl in zip(r_p, k_p, v_p, sls)]
    yc = [x - m for x, m in zip(y, mean)]
    var = [_seg_sum(x * x, ones_bd) * (1.0 / RW_HD) for x in yc]
    gate = col("g")
    for i, ((s, _), sl) in enumerate(zip(units, sls)):
        yn = yc[i] * lax.rsqrt(var[i] + LNX_EPS) * lg_ref[:, sl] + lb_ref[:, sl]
        ob_ref[s, :, sl] = ((yn + bonus[i]) * gate[i]).astype(ob_ref.dtype)

    @pl.when(c == pl.num_programs(1) - 1)
    def _():
        sout_ref[...] = s_scr[...]


RW_GROUP = 2


def _rwkv(rkv, lora, prev_rkv, prev_lora, s0_bd, w, nb, seq):
    nc = seq // CHUNK
    npair = RW_HEADS // 2
    group = RW_GROUP if nb % RW_GROUP == 0 else 1
    tok = lambda width: pl.BlockSpec((group, CHUNK, width), lambda b, c: (b, c, 0))
    per_seq = lambda width: pl.BlockSpec((group, 1, width), lambda b, c: (b, 0, 0))
    state = pl.BlockSpec((group, npair, PAIR, PAIR), lambda b, c: (b, 0, 0, 0))
    vec = _const_spec((1, RW_WIDTH))
    o_b, s_new = pl.pallas_call(
        functools.partial(_rwkv_kernel, group=group),
        out_shape=(jax.ShapeDtypeStruct((nb, seq, RW_WIDTH), BF16),
                   jax.ShapeDtypeStruct((nb, npair, PAIR, PAIR), F32)),
        grid=(nb // group, nc),
        in_specs=[tok(3 * RW_WIDTH), tok(RW_LORA_PAD), per_seq(3 * RW_WIDTH), per_seq(RW_LORA_PAD), state,
                  _const_spec((1, 3 * RW_WIDTH)), _const_spec((1, RW_LORA_PAD)),
                  _const_spec((RW_LORA_PAD, 3 * RW_WIDTH)), vec, vec, vec, vec, vec, vec, vec],
        out_specs=(tok(RW_WIDTH), state),
        scratch_shapes=[pltpu.VMEM((group, npair, PAIR, PAIR), F32), pltpu.VMEM((group, 1, 3 * RW_WIDTH), F32),
                        pltpu.VMEM((group, 1, RW_LORA_PAD), F32)],
        compiler_params=_params("parallel", "arbitrary"),
        name="rwkv",
    )(rkv.reshape(nb, seq, -1), lora.reshape(nb, seq, -1), prev_rkv, prev_lora, s0_bd,
      w["mu_rkv"], w["mu_lora"], w["w2cat"], w["w0"], w["a0"], w["k_k"], w["k_a"], w["r_k"], w["lnx_g"], w["lnx_b"])
    return o_b.reshape(nb * seq, RW_WIDTH), s_new


def _merge_kernel(x_ref, oa_ref, ob_ref, ga_ref, gb_ref, wa_ref, wb_ref, wo_ref, o_ref):
    merged = (ga_ref[...].astype(F32) * _dot(oa_ref[...], wa_ref[...])
              + gb_ref[...].astype(F32) * _dot(ob_ref[...], wb_ref[...]))
    o_ref[...] = x_ref[...] + _dot(_bf(merged), wo_ref[...])


def _merge(x, oa, ob, ga, gb, wa, wb, wo, *, tm=256):
    n = x.shape[0]
    tm = min(tm, n)
    row = lambda w: pl.BlockSpec((tm, w), lambda i: (i, 0))
    return pl.pallas_call(
        _merge_kernel,
        out_shape=jax.ShapeDtypeStruct((n, D_MODEL), F32),
        grid=(n // tm,),
        in_specs=[row(D_MODEL), row(DA_WIDTH), row(RW_WIDTH), row(D_MODEL), row(D_MODEL),
                  _const_spec((DA_WIDTH, D_MODEL)), _const_spec((RW_WIDTH, D_MODEL)),
                  _const_spec((D_MODEL, D_MODEL))],
        out_specs=row(D_MODEL),
        compiler_params=_params("parallel"),
        name="merge",
    )(x, oa, ob, ga, gb, wa, wb, wo)


def _prep_weights(l, p):
    row = lambda v: v.reshape(1, -1).astype(F32)
    w_in = p["w_in"][l]
    o_rw = 3 * DA_WIDTH
    o_g = o_rw + RW_PROJ
    w_lora = jnp.pad(w_in[:, o_rw + 3 * RW_WIDTH:o_g], ((0, 0), (0, RW_LORA_PAD - RW_LORA)))
    w2cat = jnp.zeros((RW_LORA_PAD, 3 * RW_WIDTH), F32)
    w2cat = w2cat.at[0:RW_DECAY_LORA, 0:RW_WIDTH].set(p["rw_w2"][l])
    w2cat = w2cat.at[RW_DECAY_LORA:RW_DECAY_LORA + RW_AAA_LORA, RW_WIDTH:2 * RW_WIDTH].set(p["rw_a2"][l])
    w2cat = w2cat.at[RW_DECAY_LORA + RW_AAA_LORA:RW_LORA, 2 * RW_WIDTH:].set(p["rw_g2"][l])
    mu = p["shift_mu"][l]
    return dict(
        ffn1_norm=row(p["ffn1_norm"][l]), ffn1_wi=_bf(p["ffn1_wi"][l]), ffn1_wo=_bf(p["ffn1_wo"][l]),
        ffn2_norm=row(p["ffn2_norm"][l]), ffn2_wi=_bf(p["ffn2_wi"][l]), ffn2_wo=_bf(p["ffn2_wo"][l]),
        mix_norm=row(p["mix_norm"][l]),
        wq=_bf(w_in[:, 0:DA_WIDTH]), wk=_bf(w_in[:, DA_WIDTH:2 * DA_WIDTH]), wv=_bf(w_in[:, 2 * DA_WIDTH:o_rw]),
        wq_t=_bf(w_in[:, 0:DA_WIDTH].T), wv_t=_bf(w_in[:, 2 * DA_WIDTH:o_rw].T),
        w_rkv=_bf(w_in[:, o_rw:o_rw + 3 * RW_WIDTH]), w_lora=_bf(w_lora),
        w_ga=_bf(w_in[:, o_g:o_g + D_MODEL]), w_gb=_bf(w_in[:, o_g + D_MODEL:]),
        lam_vecs=tuple(row(p[n][l]) for n in ("lambda_q1", "lambda_k1", "lambda_q2", "lambda_k2")),
        subln_g=row(p["subln_g"][l]),
        mu_rkv=row(mu[:3 * RW_WIDTH]), mu_lora=row(jnp.pad(mu[3 * RW_WIDTH:], (0, RW_LORA_PAD - RW_LORA))),
        w2cat=_bf(w2cat), w0=row(p["rw_w0"][l]), a0=row(p["rw_a0"][l]), k_k=row(p["rw_k_k"][l]),
        k_a=row(p["rw_k_a"][l]), r_k=row(p["rw_r_k"][l]), lnx_g=row(p["rw_lnx_g"][l]), lnx_b=row(p["rw_lnx_b"][l]),
        w_proj_a=_bf(p["w_proj_a"][l]), w_proj_b=_bf(p["w_proj_b"][l]), w_out=_bf(p["w_out"][l]),
    )


def _state_to_pairs(s):
    nb = s.shape[0]
    s = s.reshape(nb, RW_HEADS // 2, 2, RW_HD, 1, RW_HD)
    sel = jnp.eye(2, dtype=s.dtype).reshape(1, 1, 2, 1, 2, 1)
    return (s * sel).reshape(nb, RW_HEADS // 2, PAIR, PAIR)


def _pairs_to_state(s):
    nb = s.shape[0]
    s = s.reshape(nb, RW_HEADS // 2, 2, RW_HD, 2, RW_HD)
    return jnp.stack([s[:, :, 0, :, 0, :], s[:, :, 1, :, 1, :]], axis=2).reshape(nb, RW_HEADS, RW_HD, RW_HD)


def _layer(x, l, nb, seq, w, cache_k, cache_v, s0, prev, final_g):
    lam_init = 0.8 - 0.6 * math.exp(-0.3 * l)
    x = _ffn(x, w["ffn1_norm"], w["ffn1_wi"], w["ffn1_wo"])
    prompt = cache_k is None
    tq = min(ATTN_TQ, seq)
    q, k, v, kb, vb = _qkv_proj(x, w["mix_norm"], w["wq_t"] if prompt else w["wq"], w["wk"], w["wv"], w["wv_t"],
                                transposed=prompt, tm=tq if prompt else min(256, nb * seq))
    rkv, lora = _rw_proj(x, w["mix_norm"], w["w_rkv"], w["w_lora"])
    ga, gb = _gate_proj(x, w["mix_norm"], w["w_ga"], w["w_gb"])
    if prompt:
        o_a = _attn_prompt(q, kb, vb, w["lam_vecs"], w["subln_g"], nb, seq, lam_init, tq=tq)
        s0 = jnp.zeros((nb, RW_HEADS, RW_HD, RW_HD), F32)
        prev = jnp.zeros((nb, 1, RW_PROJ), F32)
    else:
        flat = lambda c: _bf(c[l]).reshape(nb, c.shape[2], DA_WIDTH)
        o_a = _attn_sample(q, kb, vb, flat(cache_k), flat(cache_v), w["lam_vecs"], w["subln_g"], nb, seq, lam_init)
    prev_rkv = prev[:, :, :3 * RW_WIDTH]
    prev_lora = jnp.pad(prev[:, :, 3 * RW_WIDTH:], ((0, 0), (0, 0), (0, RW_LORA_PAD - RW_LORA)))
    o_b, s_new = _rwkv(rkv, lora, prev_rkv, prev_lora, _state_to_pairs(s0), w, nb, seq)
    x = _merge(x, o_a, o_b, ga, gb, w["w_proj_a"], w["w_proj_b"], w["w_out"])
    x = _ffn(x, w["ffn2_norm"], w["ffn2_wi"], w["ffn2_wo"], final_g)
    last = jnp.concatenate([rkv.reshape(nb, seq, -1)[:, -1:], lora.reshape(nb, seq, -1)[:, -1:, :RW_LORA]], axis=-1)
    return (x, k.reshape(nb, seq, DA_HEADS, 2 * DA_HD), v.reshape(nb, seq, DA_HEADS, 2 * DA_HD),
            _pairs_to_state(s_new), last)


def _run_stream(x, weights, final_norm, caches):
    nb, seq, _ = x.shape
    x = x.reshape(nb * seq, D_MODEL)
    ks, vs, ss, shs = [], [], [], []
    depth = len(weights)
    for l, w in enumerate(weights):
        final_g = final_norm.reshape(1, -1) if l == depth - 1 else None
        if caches is None:
            ck = cv = s0 = prev = None
        else:
            ck, cv, s0, prev = caches[0], caches[1], caches[2][l], caches[3][l]
        x, k, v, s, sh = _layer(x, l, nb, seq, w, ck, cv, s0, prev, final_g)
        ks.append(k); vs.append(v); ss.append(s); shs.append(sh)
    return x.reshape(nb, seq, D_MODEL), jnp.stack(ks), jnp.stack(vs), jnp.stack(ss), jnp.stack(shs)


def kernel(x_prompt, x_sample, cache_k, cache_v, state_rwkv, state_shift, ffn1_norm, ffn1_wi, ffn1_wo, mix_norm, w_in, lambda_q1, lambda_k1, lambda_q2, lambda_k2, subln_g, shift_mu, rw_w0, rw_w2, rw_a0, rw_a2, rw_g2, rw_k_k, rw_k_a, rw_r_k, rw_lnx_g, rw_lnx_b, w_proj_a, w_proj_b, w_out, ffn2_norm, ffn2_wi, ffn2_wo, final_norm):
    params = dict(ffn1_norm=ffn1_norm, ffn1_wi=ffn1_wi, ffn1_wo=ffn1_wo, mix_norm=mix_norm, w_in=w_in,
                  lambda_q1=lambda_q1, lambda_k1=lambda_k1, lambda_q2=lambda_q2, lambda_k2=lambda_k2,
                  subln_g=subln_g, shift_mu=shift_mu, rw_w0=rw_w0, rw_w2=rw_w2, rw_a0=rw_a0, rw_a2=rw_a2,
                  rw_g2=rw_g2, rw_k_k=rw_k_k, rw_k_a=rw_k_a, rw_r_k=rw_r_k, rw_lnx_g=rw_lnx_g, rw_lnx_b=rw_lnx_b,
                  w_proj_a=w_proj_a, w_proj_b=w_proj_b, w_out=w_out, ffn2_norm=ffn2_norm, ffn2_wi=ffn2_wi,
                  ffn2_wo=ffn2_wo)
    weights = [_prep_weights(l, params) for l in range(ffn1_wi.shape[0])]
    y_p, k_p, v_p, s_p, sh_p = _run_stream(x_prompt, weights, final_norm, None)
    y_s, k_s, v_s, s_s, sh_s = _run_stream(x_sample, weights, final_norm,
                                           (cache_k, cache_v, state_rwkv, state_shift))
    return (y_p, y_s, k_p, v_p, s_p, sh_p, k_s, v_s, s_s, sh_s)
```

```python
import functools
import math

import jax
import jax.numpy as jnp
from jax import lax
from jax.experimental import pallas as pl
from jax.experimental.pallas import tpu as pltpu

F32 = jnp.float32
BF16 = jnp.bfloat16

D_MODEL = 2048
DA_HEADS = 8
DA_HD = 64
DA_WIDTH = DA_HEADS * 2 * DA_HD
RW_HEADS = 16
RW_HD = 64
RW_WIDTH = RW_HEADS * RW_HD
RW_DECAY_LORA = 96
RW_AAA_LORA = 96
RW_GATE_LORA = 256
RW_LORA = RW_DECAY_LORA + RW_AAA_LORA + RW_GATE_LORA
RW_LORA_PAD = 512
RW_PROJ = 3 * RW_WIDTH + RW_LORA
FFN_DIM = 11 * D_MODEL // 4
CHUNK = 64
NORM_EPS = 1e-6
SUBLN_EPS = 1e-5
LNX_EPS = 64e-5
NEG_BIG = -1e30

LANES = 128
BF16_ROWS = 16
PAIR = 2 * RW_HD
VMEM_LIMIT = 56 * 1024 * 1024

NT_DIMS = (((1,), (1,)), ((), ()))
TN_DIMS = (((0,), (0,)), ((), ()))


def _dot(a, b):
    return jnp.dot(a, b, preferred_element_type=F32)


def _dot_nt(a, b):
    return lax.dot_general(a, b, NT_DIMS, preferred_element_type=F32)


def _dot_tn(a, b):
    return lax.dot_general(a, b, TN_DIMS, preferred_element_type=F32)


def _bf(x):
    return x.astype(BF16)


def _rms(x, g, eps):
    return x * lax.rsqrt(jnp.mean(x * x, axis=-1, keepdims=True) + eps) * g


def _params(*sem):
    return pltpu.CompilerParams(dimension_semantics=sem, vmem_limit_bytes=VMEM_LIMIT)


def _const_spec(shape):
    nd = len(shape)
    return pl.BlockSpec(shape, lambda *_: (0,) * nd, pipeline_mode=pl.Buffered(1))


def _ffn_kernel(*refs, final):
    if final:
        x_ref, g_ref, wg_ref, wu_ref, wo_ref, fg_ref, o_ref, h_scr, acc_scr = refs
    else:
        x_ref, g_ref, wg_ref, wu_ref, wo_ref, o_ref, h_scr, acc_scr = refs
    f = pl.program_id(1)

    @pl.when(f == 0)
    def _():
        h_scr[...] = _bf(_rms(x_ref[...], g_ref[...], NORM_EPS))
        acc_scr[...] = jnp.zeros_like(acc_scr)

    h = h_scr[...]
    gate = _dot(h, wg_ref[...])
    up = _dot(h, wu_ref[...])
    act = _bf(gate * jax.nn.sigmoid(gate) * up)
    acc_scr[...] += _dot(act, wo_ref[...])

    @pl.when(f == pl.num_programs(1) - 1)
    def _():
        y = x_ref[...] + 0.5 * acc_scr[...]
        if final:
            y = _rms(y, fg_ref[...], NORM_EPS)
        o_ref[...] = y


def _ffn(x, g, wi, wo, final_g=None, *, tm=512, tf=512):
    n = x.shape[0]
    tm = min(tm, n)
    nf = FFN_DIM // tf
    final = final_g is not None
    in_specs = [
        pl.BlockSpec((tm, D_MODEL), lambda i, f: (i, 0)),
        _const_spec((1, D_MODEL)),
        pl.BlockSpec((D_MODEL, tf), lambda i, f: (0, f)),
        pl.BlockSpec((D_MODEL, tf), lambda i, f: (0, f + nf)),
        pl.BlockSpec((tf, D_MODEL), lambda i, f: (f, 0)),
    ]
    args = [x, g, wi, wi, wo]
    if final:
        in_specs.append(_const_spec((1, D_MODEL)))
        args.append(final_g)
    return pl.pallas_call(
        functools.partial(_ffn_kernel, final=final),
        out_shape=jax.ShapeDtypeStruct((n, D_MODEL), F32),
        grid=(n // tm, nf),
        in_specs=in_specs,
        out_specs=pl.BlockSpec((tm, D_MODEL), lambda i, f: (i, 0)),
        scratch_shapes=[pltpu.VMEM((tm, D_MODEL), BF16), pltpu.VMEM((tm, D_MODEL), F32)],
        compiler_params=_params("parallel", "arbitrary"),
        name="ffn_final" if final else "ffn",
    )(*args)


def _qkv_kernel(x_ref, g_ref, wq_ref, wk_ref, wv_ref, wvt_ref, q_ref, k_ref, v_ref, kb_ref, vb_ref, *, transposed):
    h = _bf(_rms(x_ref[...], g_ref[...], NORM_EPS))
    q = _dot_nt(wq_ref[...], h) if transposed else _dot(h, wq_ref[...])
    q_ref[...] = _bf(q * (DA_HD ** -0.5))
    k = _dot(h, wk_ref[...])
    k_ref[...] = k
    kb_ref[...] = _bf(k)
    v = _dot(h, wv_ref[...])
    v_ref[...] = v
    vb_ref[...] = _bf(_dot_nt(wvt_ref[...], h)) if transposed else _bf(v)


def _qkv_proj(x, g, wq, wk, wv, wvt, *, transposed, tm):
    n = x.shape[0]
    row = lambda w: pl.BlockSpec((tm, w), lambda i: (i, 0))
    rows = lambda w, dt: jax.ShapeDtypeStruct((n, w), dt)
    wspec = _const_spec(wk.shape)
    if transposed:
        slab = pl.BlockSpec((None, DA_WIDTH, tm), lambda i: (i, 0, 0))
        slabs = jax.ShapeDtypeStruct((n // tm, DA_WIDTH, tm), BF16)
        q_spec, q_shape, vb_spec, vb_shape = slab, slabs, slab, slabs
    else:
        q_spec, q_shape, vb_spec, vb_shape = row(DA_WIDTH), rows(DA_WIDTH, BF16), row(DA_WIDTH), rows(DA_WIDTH, BF16)
    return pl.pallas_call(
        functools.partial(_qkv_kernel, transposed=transposed),
        out_shape=(q_shape, rows(DA_WIDTH, F32), rows(DA_WIDTH, F32), rows(DA_WIDTH, BF16), vb_shape),
        grid=(n // tm,),
        in_specs=[row(D_MODEL), _const_spec((1, D_MODEL)), _const_spec(wq.shape), wspec, wspec,
                  _const_spec(wvt.shape)],
        out_specs=(q_spec, row(DA_WIDTH), row(DA_WIDTH), row(DA_WIDTH), vb_spec),
        compiler_params=_params("parallel"),
        name="qkv_proj_t" if transposed else "qkv_proj",
    )(x, g, wq, wk, wv, wvt)


def _rw_proj_kernel(x_ref, g_ref, wrkv_ref, wl_ref, rkv_ref, lora_ref):
    h = _bf(_rms(x_ref[...], g_ref[...], NORM_EPS))
    rkv_ref[...] = _dot(h, wrkv_ref[...])
    lora_ref[...] = _dot(h, wl_ref[...])


def _rw_proj(x, g, wrkv, wl, *, tm=256):
    n = x.shape[0]
    tm = min(tm, n)
    row = lambda w: pl.BlockSpec((tm, w), lambda i: (i, 0))
    return pl.pallas_call(
        _rw_proj_kernel,
        out_shape=(jax.ShapeDtypeStruct((n, 3 * RW_WIDTH), F32),
                   jax.ShapeDtypeStruct((n, RW_LORA_PAD), F32)),
        grid=(n // tm,),
        in_specs=[row(D_MODEL), _const_spec((1, D_MODEL)),
                  _const_spec((D_MODEL, 3 * RW_WIDTH)), _const_spec((D_MODEL, RW_LORA_PAD))],
        out_specs=(row(3 * RW_WIDTH), row(RW_LORA_PAD)),
        compiler_params=_params("parallel"),
        name="rw_proj",
    )(x, g, wrkv, wl)


def _gate_kernel(x_ref, g_ref, wa_ref, wb_ref, ga_ref, gb_ref):
    h = _bf(_rms(x_ref[...], g_ref[...], NORM_EPS))
    ga_ref[...] = _bf(jax.nn.sigmoid(_dot(h, wa_ref[...])))
    gb_ref[...] = _bf(jax.nn.sigmoid(_dot(h, wb_ref[...])))


def _gate_proj(x, g, wa, wb, *, tm=256):
    n = x.shape[0]
    tm = min(tm, n)
    row = pl.BlockSpec((tm, D_MODEL), lambda i: (i, 0))
    wspec = _const_spec((D_MODEL, D_MODEL))
    return pl.pallas_call(
        _gate_kernel,
        out_shape=(jax.ShapeDtypeStruct((n, D_MODEL), BF16),) * 2,
        grid=(n // tm,),
        in_specs=[row, _const_spec((1, D_MODEL)), wspec, wspec],
        out_specs=(row, row),
        compiler_params=_params("parallel"),
        name="gate_proj",
    )(x, g, wa, wb)


def _stack_maps(q):
    lane = lax.broadcasted_iota(jnp.int32, q.shape, 1)
    zero = jnp.zeros_like(q)
    return jnp.concatenate([jnp.where(lane < DA_HD, q, zero), jnp.where(lane >= DA_HD, q, zero)], axis=0)


def _online_softmax_step(qst, kb, vb, m_scr, l_scr, acc_scr, mask=None):
    s = _dot_nt(qst, kb)
    if mask is not None:
        s = jnp.where(mask, s, NEG_BIG)
    m_prev = m_scr[...]
    m_next = jnp.maximum(m_prev, jnp.max(s, axis=1, keepdims=True))
    alpha = jnp.exp(m_prev - m_next)
    p = jnp.exp(s - m_next)
    l_scr[...] = alpha * l_scr[...] + jnp.sum(p, axis=1, keepdims=True)
    acc_scr[...] = alpha * acc_scr[...] + _dot(_bf(p), vb)
    m_scr[...] = m_next


def _attn_init(m_scr, l_scr, acc_scr):
    m_scr[...] = jnp.full_like(m_scr, NEG_BIG)
    l_scr[...] = jnp.zeros_like(l_scr)
    acc_scr[...] = jnp.zeros_like(acc_scr)


def _attn_finalize(tq, lam_refs, sg_ref, o_ref, l_scr, acc_scr, lam_init):
    lq1, lk1, lq2, lk2 = lam_refs
    lam = (jnp.exp(jnp.sum(lq1[...] * lk1[...], axis=-1, keepdims=True))
           - jnp.exp(jnp.sum(lq2[...] * lk2[...], axis=-1, keepdims=True)) + lam_init)
    o = acc_scr[...] / l_scr[...]
    o = o[:tq] - lam * o[tq:]
    o = o * lax.rsqrt(jnp.mean(o * o, axis=-1, keepdims=True) + SUBLN_EPS)
    o_ref[...] = (o * sg_ref[...] * (1.0 - lam_init)).astype(o_ref.dtype)


ATTN_STRIP = 256
ATTN_TQ = 512


def _attn_prompt_kernel(qt_ref, k_ref, vt_ref, lq1, lk1, lq2, lk2, sg_ref, o_ref,
                        qst_scr, m_scr, acc_scr, s_scr, p_scr, alpha_scr, *, tq, lam_init):
    qi = pl.program_id(2)
    n_strips = 2 * tq // ATTN_STRIP
    last = n_strips - 1
    strip = lambda c: slice(c * ATTN_STRIP, (c + 1) * ATTN_STRIP)
    chunks = tq // CHUNK

    qt = qt_ref[...]
    feat = lax.broadcasted_iota(jnp.int32, qt.shape, 0)
    zero = jnp.zeros_like(qt)
    qst_scr[:, 0:tq] = jnp.where(feat < DA_HD, qt, zero)
    qst_scr[:, tq:2 * tq] = jnp.where(feat >= DA_HD, qt, zero)
    m_scr[...] = jnp.full_like(m_scr, NEG_BIG)
    acc_scr[...] = jnp.zeros_like(acc_scr)
    p_scr[...] = jnp.zeros_like(p_scr)
    alpha_scr[...] = jnp.ones_like(alpha_scr)

    def key_block(j):
        return k_ref[pl.ds(pl.multiple_of(j * tq, tq), tq), :]

    def value_block(j):
        return jnp.concatenate([vt_ref[j], jnp.ones((BF16_ROWS, tq), BF16)], axis=0)

    def chunk_gap(c):
        kc = lax.broadcasted_iota(jnp.int32, (tq, ATTN_STRIP), 0) // CHUNK
        qc = ((lax.broadcasted_iota(jnp.int32, (tq, ATTN_STRIP), 1) + c * ATTN_STRIP) % tq) // CHUNK
        return kc - qc

    def scores(kb, c, blocks_below_diag=None):
        s = _dot(kb, qst_scr[:, strip(c)])
        if blocks_below_diag is not None:
            s = jnp.where(chunk_gap(c) <= blocks_below_diag * chunks, s, NEG_BIG)
        return s

    def softmax(c, s):
        cs = strip(c)
        m_prev = m_scr[:, cs]
        m_next = jnp.maximum(m_prev, jnp.max(s, axis=0, keepdims=True))
        m_scr[:, cs] = m_next
        return _bf(jnp.exp(s - m_next)), jnp.exp(m_prev - m_next)

    def accumulate(c, vt, p, alpha):
        cs = strip(c)
        acc_scr[:, cs] = alpha * acc_scr[:, cs] + _dot(vt, p)

    def run_block(j, vt_prev, diag):
        kb = key_block(j)
        vt = value_block(j)
        mask = 0 if diag else None
        s_next = scores(kb, 1, mask)
        p, alpha = softmax(0, s_scr[...])
        accumulate(last, vt_prev, p_scr[...], alpha_scr[...])
        for c in range(1, n_strips):
            s = s_next
            if c < last:
                s_next = scores(kb, c + 1, mask)
            elif not diag:
                s_scr[...] = scores(key_block(j + 1), 0, qi - (j + 1))
            accumulate(c - 1, vt, p, alpha)
            p, alpha = softmax(c, s)
        return vt, p, alpha

    s_scr[...] = scores(key_block(0), 0, qi)

    def body(j, carry):
        vt_prev = value_block(jnp.maximum(j - 1, 0))
        _, p, alpha = run_block(j, vt_prev, False)
        p_scr[...] = p
        alpha_scr[...] = alpha
        return carry

    lax.fori_loop(0, qi, body, 0)
    vt, p, alpha = run_block(qi, value_block(jnp.maximum(qi - 1, 0)), True)
    accumulate(last, vt, p, alpha)

    lam = (jnp.exp(jnp.sum(lq1[...] * lk1[...], axis=-1, keepdims=True))
           - jnp.exp(jnp.sum(lq2[...] * lk2[...], axis=-1, keepdims=True)) + lam_init)
    o = acc_scr[0:LANES, :] / acc_scr[LANES:LANES + 1, :]
    o = o[:, :tq] - lam * o[:, tq:]
    o = o * lax.rsqrt(jnp.mean(o * o, axis=0, keepdims=True) + SUBLN_EPS)
    o = o * sg_ref[...] * (1.0 - lam_init)
    o_ref[...] = o.T.astype(o_ref.dtype)


def _attn_prompt(qt, kb, vt, lam_vecs, subln_g, nb, seq, lam_init, *, tq):
    nq = seq // tq
    vec = _const_spec((1, DA_HD))
    return pl.pallas_call(
        functools.partial(_attn_prompt_kernel, tq=tq, lam_init=lam_init),
        out_shape=jax.ShapeDtypeStruct((nb * seq, DA_WIDTH), BF16),
        grid=(nb, DA_HEADS, nq),
        in_specs=[pl.BlockSpec((None, LANES, tq), lambda b, h, i: (b * nq + i, h, 0)),
                  pl.BlockSpec((seq, LANES), lambda b, h, i: (b, h)),
                  pl.BlockSpec((nq, LANES, tq), lambda b, h, i: (b, h, 0)),
                  vec, vec, vec, vec, _const_spec((LANES, 1))],
        out_specs=pl.BlockSpec((tq, LANES), lambda b, h, i: (b * nq + i, h)),
        scratch_shapes=[pltpu.VMEM((LANES, 2 * tq), BF16), pltpu.VMEM((1, 2 * tq), F32),
                        pltpu.VMEM((LANES + BF16_ROWS, 2 * tq), F32), pltpu.VMEM((tq, ATTN_STRIP), F32),
                        pltpu.VMEM((tq, ATTN_STRIP), BF16), pltpu.VMEM((1, ATTN_STRIP), F32)],
        compiler_params=_params("parallel", "parallel", "arbitrary"),
        name="attn_prompt",
    )(qt, kb, vt, *lam_vecs, subln_g.reshape(LANES, 1))


def _attn_sample_kernel(q_ref, kn_ref, vn_ref, kc_hbm, vc_hbm, lq1, lk1, lq2, lk2, sg_ref, o_ref,
                        kbuf, vbuf, sem, qst_scr, m_scr, l_scr, acc_scr, *, tq, tk, n_blk, row0, lam_init):
    row = row0 + pl.program_id(0)
    heads = [slice(h * LANES, (h + 1) * LANES) for h in range(DA_HEADS)]
    state = lambda h: (m_scr.at[h], l_scr.at[h], acc_scr.at[h])
    n_items = DA_HEADS * n_blk

    def copies(w, slot):
        h = w // n_blk
        start = pl.multiple_of((w % n_blk) * tk, tk)
        return [pltpu.make_async_copy(c.at[row, pl.ds(start, tk), h, :], buf.at[slot], sem.at[i, slot])
                for i, (c, buf) in enumerate(((kc_hbm, kbuf), (vc_hbm, vbuf)))]

    for cp in copies(0, 0):
        cp.start()

    for h, cols in enumerate(heads):
        qst_scr[h] = _stack_maps(q_ref[:, cols])
        _attn_init(*state(h))
        _online_softmax_step(qst_scr[h], kn_ref[:, cols], vn_ref[:, cols], *state(h))

    def body(w, carry):
        slot = w % 2

        @pl.when(w + 1 < n_items)
        def _():
            for cp in copies(w + 1, 1 - slot):
                cp.start()

        for cp in copies(w, slot):
            cp.wait()
        h = w // n_blk
        _online_softmax_step(qst_scr[h], _bf(kbuf[slot]), _bf(vbuf[slot]), *state(h))
        return carry

    lax.fori_loop(0, n_items, body, 0)

    for h, cols in enumerate(heads):
        _attn_finalize(tq, (lq1, lk1, lq2, lk2), sg_ref, o_ref.at[:, cols], l_scr.at[h], acc_scr.at[h], lam_init)


def _attn_sample(q, kb, vb, cache_k, cache_v, layer, lam_vecs, subln_g, nb, seq, lam_init, *, tk=1024):
    past = cache_k.shape[1]
    tk = min(tk, past)
    qspec = pl.BlockSpec((seq, DA_WIDTH), lambda b: (b, 0))
    hbm = pl.BlockSpec(memory_space=pl.ANY)
    vec = _const_spec((1, DA_HD))
    return pl.pallas_call(
        functools.partial(_attn_sample_kernel, tq=seq, tk=tk, n_blk=past // tk, row0=layer * nb, lam_init=lam_init),
        out_shape=jax.ShapeDtypeStruct((nb * seq, DA_WIDTH), BF16),
        grid=(nb,),
        in_specs=[qspec, qspec, qspec, hbm, hbm, vec, vec, vec, vec, _const_spec((1, LANES))],
        out_specs=qspec,
        scratch_shapes=[pltpu.VMEM((2, tk, LANES), F32), pltpu.VMEM((2, tk, LANES), F32),
                        pltpu.SemaphoreType.DMA((2, 2)),
                        pltpu.VMEM((DA_HEADS, 2 * seq, LANES), BF16), pltpu.VMEM((DA_HEADS, 2 * seq, 1), F32),
                        pltpu.VMEM((DA_HEADS, 2 * seq, 1), F32), pltpu.VMEM((DA_HEADS, 2 * seq, LANES), F32)],
        compiler_params=_params("arbitrary"),
        name="attn_sample",
    )(q, kb, vb, cache_k, cache_v, *lam_vecs, subln_g)


def _split3(x):
    hi = _bf(x)
    r1 = x - hi.astype(F32)
    mid = _bf(r1)
    lo = _bf(r1 - mid.astype(F32))
    return hi, mid, lo


def _seg_sum(x, ones_bd):
    hi = _bf(x)
    lo = _bf(x - hi.astype(F32))
    return _dot(hi, ones_bd) + _dot(lo, ones_bd)


def _stack_heads(x):
    lane = lax.broadcasted_iota(jnp.int32, x.shape, 1)
    zero = jnp.zeros_like(x)
    return _bf(jnp.concatenate([jnp.where(lane < RW_HD, x, zero), jnp.where(lane >= RW_HD, x, zero)], axis=0))


def _unit_lower_inverse(a_list, eye, blk16, blk32):
    a16 = [jnp.where(blk16, a, 0.0) for a in a_list]
    a32 = [jnp.where(blk32, a, 0.0) - d for a, d in zip(a_list, a16)]
    a64 = [a - d - e for a, d, e in zip(a_list, a16, a32)]
    t = [eye - d for d in a16]
    q = a16
    for _ in range(3):
        qb = [_bf(x) for x in q]
        q = [_dot(x, x) for x in qb]
        t = [x + _dot(_bf(x), _bf(y)) for x, y in zip(t, q)]
    for off in (a32, a64):
        tb = [_bf(x) for x in t]
        inner = [_bf(_dot(_bf(o), x)) for o, x in zip(off, tb)]
        t = [x - _dot(xb, i) for x, xb, i in zip(t, tb, inner)]
    return t


def _rwkv_kernel(rkv_ref, lora_ref, prkv_ref, plora_ref, s0_ref, mu_rkv_ref, mu_lora_ref, w2_ref,
                 w0_ref, a0_ref, kk_ref, ka_ref, rk_ref, lg_ref, lb_ref,
                 ob_ref, sout_ref, s_scr, prev_rkv_scr, prev_lora_scr, *, group):
    c = pl.program_id(1)

    @pl.when(c == 0)
    def _():
        s_scr[...] = s0_ref[...]
        prev_rkv_scr[...] = prkv_ref[...]
        prev_lora_scr[...] = plora_ref[...]

    row = lax.broadcasted_iota(jnp.int32, (CHUNK, 1), 0)
    ti = lax.broadcasted_iota(jnp.int32, (CHUNK, CHUNK), 0)
    si = lax.broadcasted_iota(jnp.int32, (CHUNK, CHUNK), 1)
    tri = jnp.where(ti >= si, 1.0, 0.0).astype(BF16)

    def token_shift(u, prev, mu):
        u_prev = jnp.where(row == 0, prev, pltpu.roll(u, 1, 0))
        return u + mu * (u_prev - u)

    def per_token(s):
        u = rkv_ref[s]
        ul = lora_ref[s]
        us = token_shift(u, prev_rkv_scr[s], mu_rkv_ref[...])
        usl = token_shift(ul, prev_lora_scr[s], mu_lora_ref[...])
        prev_rkv_scr[s] = u[CHUNK - 1:CHUNK, :]
        prev_lora_scr[s] = ul[CHUNK - 1:CHUNK, :]
        lane_l = lax.broadcasted_iota(jnp.int32, usl.shape, 1)
        act = jnp.where(lane_l < RW_DECAY_LORA, jnp.tanh(usl),
                        jnp.where(lane_l < RW_DECAY_LORA + RW_AAA_LORA, usl, jax.nn.sigmoid(usl)))
        lin = _dot(_bf(act), w2_ref[...])
        wx = -(w0_ref[...] + lin[:, 0:RW_WIDTH])
        w = -(jnp.maximum(wx, 0.0) + jnp.log(1.0 + jnp.exp(-jnp.abs(wx)))) - 0.5
        log_decay = -jnp.exp(w)
        cs = sum(_dot(tri, part) for part in _split3(log_decay))
        return dict(r=us[:, 0:RW_WIDTH], k=us[:, RW_WIDTH:2 * RW_WIDTH], v=us[:, 2 * RW_WIDTH:3 * RW_WIDTH],
                    a=jax.nn.sigmoid(a0_ref[...] + lin[:, RW_WIDTH:2 * RW_WIDTH]),
                    g=lin[:, 2 * RW_WIDTH:3 * RW_WIDTH], cs=cs, log_decay=log_decay)

    tok = [per_token(s) for s in range(group)]

    ri = lax.broadcasted_iota(jnp.int32, (PAIR, PAIR), 0)
    ci = lax.broadcasted_iota(jnp.int32, (PAIR, PAIR), 1)
    eye = jnp.where(ri == ci, 1.0, 0.0).astype(F32)
    same_head = (ri // RW_HD) == (ci // RW_HD)
    ones_bd = jnp.where(same_head, 1.0, 0.0).astype(BF16)
    strict = same_head & (ri > ci)
    incl = same_head & (ri >= ci)
    blk16 = (ri // 16) == (ci // 16)
    blk32 = (ri // 32) == (ci // 32)

    units = [(s, p) for s in range(group) for p in range(RW_HEADS // 2)]
    sls = [slice(p * PAIR, (p + 1) * PAIR) for _, p in units]
    col = lambda name: [tok[s][name][:, sl] for (s, _), sl in zip(units, sls)]
    cat = lambda xs, ys: [jnp.concatenate([x, y], axis=0) for x, y in zip(xs, ys)]

    k_raw, a_p, r_p, v_p, cs, log_decay = col("k"), col("a"), col("r"), col("v"), col("cs"), col("log_decay")
    kk = [x * kk_ref[:, sl] for x, sl in zip(k_raw, sls)]
    kk_ss = [_seg_sum(x * x, ones_bd) for x in kk]
    kk = [x * lax.rsqrt(jnp.maximum(ss, 1e-24)) for x, ss in zip(kk, kk_ss)]
    k_p = [x * (1.0 + (ap - 1.0) * ka_ref[:, sl]) for x, ap, sl in zip(k_raw, a_p, sls)]
    g_fwd = [jnp.exp(x) for x in cs]
    g_inv = [jnp.exp(-x) for x in cs]
    g_prev = [jnp.exp(x - ld) for x, ld in zip(cs, log_decay)]

    ka_s = [_stack_heads(x * gp) for x, gp in zip(kk, g_prev)]
    b_s = [_stack_heads(x * ap * gi) for x, ap, gi in zip(kk, a_p, g_inv)]
    k_s = [_stack_heads(x * gi) for x, gi in zip(k_p, g_inv)]
    r_s = [_stack_heads(x * gf) for x, gf in zip(r_p, g_fwd)]
    v_s = [_stack_heads(x) for x in v_p]
    bk_s = cat(b_s, k_s)

    a_all = [_dot_nt(x, y) for x, y in zip(cat(ka_s, r_s), bk_s)]
    a_ab = [jnp.where(strict, x[:PAIR, :PAIR], 0.0) for x in a_all]
    a_ak = [_bf(jnp.where(strict, x[:PAIR, PAIR:], 0.0)) for x in a_all]
    a_rb = [_bf(jnp.where(incl, x[PAIR:, :PAIR], 0.0)) for x in a_all]
    a_rk = [_bf(jnp.where(incl, x[PAIR:, PAIR:], 0.0)) for x in a_all]
    t_inv = [_bf(t) for t in _unit_lower_inverse(a_ab, eye, blk16, blk32)]

    s_old = [s_scr[s, p] for s, p in units]
    s_b = [_bf(x) for x in s_old]
    rhs_s = [_dot_nt(x, sb) for x, sb in zip(ka_s, s_b)]
    rhs_v = [_dot(x, vs) for x, vs in zip(a_ak, v_s)]
    sa_b = [_bf(_dot(t, _bf(-(x + y)))) for t, x, y in zip(t_inv, rhs_s, rhs_v)]
    y_s = [_dot_nt(x, sb) for x, sb in zip(r_s, s_b)]
    y_v = [_dot(x, vs) for x, vs in zip(a_rk, v_s)]
    y_sa = [_dot(x, sa) for x, sa in zip(a_rb, sa_b)]
    upd = [_dot_tn(x, y) for x, y in zip(cat(sa_b, v_s), bk_s)]
    for i, (s, p) in enumerate(units):
        s_scr[s, p] = g_fwd[i][CHUNK - 1:CHUNK, :] * (s_old[i] + upd[i])

    y_bd = [x + y + z for x, y, z in zip(y_s, y_v, y_sa)]
    y = [x[:CHUNK] + x[CHUNK:] for x in y_bd]
    mean = [_seg_sum(x, ones_bd) * (1.0 / RW_HD) for x in y]
    bonus = [_seg_sum(rp * kp * rk_ref[:, sl], ones_bd) * vp for rp, kp, vp, sl in zip(r_p, k_p, v_p, sls)]
    yc = [x - m for x, m in zip(y, mean)]
    var = [_seg_sum(x * x, ones_bd) * (1.0 / RW_HD) for x in yc]
    gate = col("g")
    for i, ((s, _), sl) in enumerate(zip(units, sls)):
        yn = yc[i] * lax.rsqrt(var[i] + LNX_EPS) * lg_ref[:, sl] + lb_ref[:, sl]
        ob_ref[s, :, sl] = ((yn + bonus[i]) * gate[i]).astype(ob_ref.dtype)

    @pl.when(c == pl.num_programs(1) - 1)
    def _():
        sout_ref[...] = s_scr[...]


RW_GROUP = 2


def _rwkv(rkv, lora, prev_rkv, prev_lora, s0_bd, w, nb, seq):
    nc = seq // CHUNK
    npair = RW_HEADS // 2
    group = RW_GROUP if nb % RW_GROUP == 0 else 1
    tok = lambda width: pl.BlockSpec((group, CHUNK, width), lambda b, c: (b, c, 0))
    per_seq = lambda width: pl.BlockSpec((group, 1, width), lambda b, c: (b, 0, 0))
    state = pl.BlockSpec((group, npair, PAIR, PAIR), lambda b, c: (b, 0, 0, 0))
    vec = _const_spec((1, RW_WIDTH))
    o_b, s_new = pl.pallas_call(
        functools.partial(_rwkv_kernel, group=group),
        out_shape=(jax.ShapeDtypeStruct((nb, seq, RW_WIDTH), BF16),
                   jax.ShapeDtypeStruct((nb, npair, PAIR, PAIR), F32)),
        grid=(nb // group, nc),
        in_specs=[tok(3 * RW_WIDTH), tok(RW_LORA_PAD), per_seq(3 * RW_WIDTH), per_seq(RW_LORA_PAD), state,
                  _const_spec((1, 3 * RW_WIDTH)), _const_spec((1, RW_LORA_PAD)),
                  _const_spec((RW_LORA_PAD, 3 * RW_WIDTH)), vec, vec, vec, vec, vec, vec, vec],
        out_specs=(tok(RW_WIDTH), state),
        scratch_shapes=[pltpu.VMEM((group, npair, PAIR, PAIR), F32), pltpu.VMEM((group, 1, 3 * RW_WIDTH), F32),
                        pltpu.VMEM((group, 1, RW_LORA_PAD), F32)],
        compiler_params=_params("parallel", "arbitrary"),
        name="rwkv",
    )(rkv.reshape(nb, seq, -1), lora.reshape(nb, seq, -1), prev_rkv, prev_lora, s0_bd,
      w["mu_rkv"], w["mu_lora"], w["w2cat"], w["w0"], w["a0"], w["k_k"], w["k_a"], w["r_k"], w["lnx_g"], w["lnx_b"])
    return o_b.reshape(nb * seq, RW_WIDTH), s_new


def _merge_kernel(x_ref, oa_ref, ob_ref, ga_ref, gb_ref, wa_ref, wb_ref, wo_ref, o_ref):
    merged = (ga_ref[...].astype(F32) * _dot(oa_ref[...], wa_ref[...])
              + gb_ref[...].astype(F32) * _dot(ob_ref[...], wb_ref[...]))
    o_ref[...] = x_ref[...] + _dot(_bf(merged), wo_ref[...])


def _merge(x, oa, ob, ga, gb, wa, wb, wo, *, tm=256):
    n = x.shape[0]
    tm = min(tm, n)
    row = lambda w: pl.BlockSpec((tm, w), lambda i: (i, 0))
    return pl.pallas_call(
        _merge_kernel,
        out_shape=jax.ShapeDtypeStruct((n, D_MODEL), F32),
        grid=(n // tm,),
        in_specs=[row(D_MODEL), row(DA_WIDTH), row(RW_WIDTH), row(D_MODEL), row(D_MODEL),
                  _const_spec((DA_WIDTH, D_MODEL)), _const_spec((RW_WIDTH, D_MODEL)),
                  _const_spec((D_MODEL, D_MODEL))],
        out_specs=row(D_MODEL),
        compiler_params=_params("parallel"),
        name="merge",
    )(x, oa, ob, ga, gb, wa, wb, wo)


def _prep_weights(l, p):
    row = lambda v: v.reshape(1, -1).astype(F32)
    w_in = p["w_in"][l]
    o_rw = 3 * DA_WIDTH
    o_g = o_rw + RW_PROJ
    w_lora = jnp.pad(w_in[:, o_rw + 3 * RW_WIDTH:o_g], ((0, 0), (0, RW_LORA_PAD - RW_LORA)))
    w2cat = jnp.zeros((RW_LORA_PAD, 3 * RW_WIDTH), F32)
    w2cat = w2cat.at[0:RW_DECAY_LORA, 0:RW_WIDTH].set(p["rw_w2"][l])
    w2cat = w2cat.at[RW_DECAY_LORA:RW_DECAY_LORA + RW_AAA_LORA, RW_WIDTH:2 * RW_WIDTH].set(p["rw_a2"][l])
    w2cat = w2cat.at[RW_DECAY_LORA + RW_AAA_LORA:RW_LORA, 2 * RW_WIDTH:].set(p["rw_g2"][l])
    mu = p["shift_mu"][l]
    return dict(
        ffn1_norm=row(p["ffn1_norm"][l]), ffn1_wi=_bf(p["ffn1_wi"][l]), ffn1_wo=_bf(p["ffn1_wo"][l]),
        ffn2_norm=row(p["ffn2_norm"][l]), ffn2_wi=_bf(p["ffn2_wi"][l]), ffn2_wo=_bf(p["ffn2_wo"][l]),
        mix_norm=row(p["mix_norm"][l]),
        wq=_bf(w_in[:, 0:DA_WIDTH]), wk=_bf(w_in[:, DA_WIDTH:2 * DA_WIDTH]), wv=_bf(w_in[:, 2 * DA_WIDTH:o_rw]),
        wq_t=_bf(w_in[:, 0:DA_WIDTH].T), wv_t=_bf(w_in[:, 2 * DA_WIDTH:o_rw].T),
        w_rkv=_bf(w_in[:, o_rw:o_rw + 3 * RW_WIDTH]), w_lora=_bf(w_lora),
        w_ga=_bf(w_in[:, o_g:o_g + D_MODEL]), w_gb=_bf(w_in[:, o_g + D_MODEL:]),
        lam_vecs=tuple(row(p[n][l]) for n in ("lambda_q1", "lambda_k1", "lambda_q2", "lambda_k2")),
        subln_g=row(p["subln_g"][l]),
        mu_rkv=row(mu[:3 * RW_WIDTH]), mu_lora=row(jnp.pad(mu[3 * RW_WIDTH:], (0, RW_LORA_PAD - RW_LORA))),
        w2cat=_bf(w2cat), w0=row(p["rw_w0"][l]), a0=row(p["rw_a0"][l]), k_k=row(p["rw_k_k"][l]),
        k_a=row(p["rw_k_a"][l]), r_k=row(p["rw_r_k"][l]), lnx_g=row(p["rw_lnx_g"][l]), lnx_b=row(p["rw_lnx_b"][l]),
        w_proj_a=_bf(p["w_proj_a"][l]), w_proj_b=_bf(p["w_proj_b"][l]), w_out=_bf(p["w_out"][l]),
    )


def _state_to_pairs(s):
    nb = s.shape[0]
    s = s.reshape(nb, RW_HEADS // 2, 2, RW_HD, 1, RW_HD)
    sel = jnp.eye(2, dtype=s.dtype).reshape(1, 1, 2, 1, 2, 1)
    return (s * sel).reshape(nb, RW_HEADS // 2, PAIR, PAIR)


def _pairs_to_state(s):
    nb = s.shape[0]
    s = s.reshape(nb, RW_HEADS // 2, 2, RW_HD, 2, RW_HD)
    return jnp.stack([s[:, :, 0, :, 0, :], s[:, :, 1, :, 1, :]], axis=2).reshape(nb, RW_HEADS, RW_HD, RW_HD)


def _layer(x, l, nb, seq, w, cache_k, cache_v, s0, prev, final_g):
    lam_init = 0.8 - 0.6 * math.exp(-0.3 * l)
    x = _ffn(x, w["ffn1_norm"], w["ffn1_wi"], w["ffn1_wo"])
    prompt = cache_k is None
    tq = min(ATTN_TQ, seq)
    q, k, v, kb, vb = _qkv_proj(x, w["mix_norm"], w["wq_t"] if prompt else w["wq"], w["wk"], w["wv"], w["wv_t"],
                                transposed=prompt, tm=tq if prompt else min(256, nb * seq))
    rkv, lora = _rw_proj(x, w["mix_norm"], w["w_rkv"], w["w_lora"])
    ga, gb = _gate_proj(x, w["mix_norm"], w["w_ga"], w["w_gb"])
    if prompt:
        o_a = _attn_prompt(q, kb, vb, w["lam_vecs"], w["subln_g"], nb, seq, lam_init, tq=tq)
        s0 = jnp.zeros((nb, RW_HEADS, RW_HD, RW_HD), F32)
        prev = jnp.zeros((nb, 1, RW_PROJ), F32)
    else:
        rows = lambda c: c.reshape((-1,) + c.shape[2:])
        o_a = _attn_sample(q, kb, vb, rows(cache_k), rows(cache_v), l, w["lam_vecs"], w["subln_g"], nb, seq,
                           lam_init)
    prev_rkv = prev[:, :, :3 * RW_WIDTH]
    prev_lora = jnp.pad(prev[:, :, 3 * RW_WIDTH:], ((0, 0), (0, 0), (0, RW_LORA_PAD - RW_LORA)))
    o_b, s_new = _rwkv(rkv, lora, prev_rkv, prev_lora, _state_to_pairs(s0), w, nb, seq)
    x = _merge(x, o_a, o_b, ga, gb, w["w_proj_a"], w["w_proj_b"], w["w_out"])
    x = _ffn(x, w["ffn2_norm"], w["ffn2_wi"], w["ffn2_wo"], final_g)
    last = jnp.concatenate([rkv.reshape(nb, seq, -1)[:, -1:], lora.reshape(nb, seq, -1)[:, -1:, :RW_LORA]], axis=-1)
    return (x, k.reshape(nb, seq, DA_HEADS, 2 * DA_HD), v.reshape(nb, seq, DA_HEADS, 2 * DA_HD),
            _pairs_to_state(s_new), last)


def _run_stream(x, weights, final_norm, caches):
    nb, seq, _ = x.shape
    x = x.reshape(nb * seq, D_MODEL)
    ks, vs, ss, shs = [], [], [], []
    depth = len(weights)
    for l, w in enumerate(weights):
        final_g = final_norm.reshape(1, -1) if l == depth - 1 else None
        if caches is None:
            ck = cv = s0 = prev = None
        else:
            ck, cv, s0, prev = caches[0], caches[1], caches[2][l], caches[3][l]
        x, k, v, s, sh = _layer(x, l, nb, seq, w, ck, cv, s0, prev, final_g)
        ks.append(k); vs.append(v); ss.append(s); shs.append(sh)
    return x.reshape(nb, seq, D_MODEL), jnp.stack(ks), jnp.stack(vs), jnp.stack(ss), jnp.stack(shs)


def kernel(x_prompt, x_sample, cache_k, cache_v, state_rwkv, state_shift, ffn1_norm, ffn1_wi, ffn1_wo, mix_norm, w_in, lambda_q1, lambda_k1, lambda_q2, lambda_k2, subln_g, shift_mu, rw_w0, rw_w2, rw_a0, rw_a2, rw_g2, rw_k_k, rw_k_a, rw_r_k, rw_lnx_g, rw_lnx_b, w_proj_a, w_proj_b, w_out, ffn2_norm, ffn2_wi, ffn2_wo, final_norm):
    params = dict(ffn1_norm=ffn1_norm, ffn1_wi=ffn1_wi, ffn1_wo=ffn1_wo, mix_norm=mix_norm, w_in=w_in,
                  lambda_q1=lambda_q1, lambda_k1=lambda_k1, lambda_q2=lambda_q2, lambda_k2=lambda_k2,
                  subln_g=subln_g, shift_mu=shift_mu, rw_w0=rw_w0, rw_w2=rw_w2, rw_a0=rw_a0, rw_a2=rw_a2,
                  rw_g2=rw_g2, rw_k_k=rw_k_k, rw_k_a=rw_k_a, rw_r_k=rw_r_k, rw_lnx_g=rw_lnx_g, rw_lnx_b=rw_lnx_b,
                  w_proj_a=w_proj_a, w_proj_b=w_proj_b, w_out=w_out, ffn2_norm=ffn2_norm, ffn2_wi=ffn2_wi,
                  ffn2_wo=ffn2_wo)
    weights = [_prep_weights(l, params) for l in range(ffn1_wi.shape[0])]
    y_p, k_p, v_p, s_p, sh_p = _run_stream(x_prompt, weights, final_norm, None)
    y_s, k_s, v_s, s_s, sh_s = _run_stream(x_sample, weights, final_norm,
                                           (cache_k, cache_v, state_rwkv, state_shift))
    return (y_p, y_s, k_p, v_p, s_p, sh_p, k_s, v_s, s_s, sh_s)
```

```python
import functools
import math

import jax
import jax.numpy as jnp
from jax import lax
from jax.experimental import pallas as pl
from jax.experimental.pallas import tpu as pltpu

F32 = jnp.float32
BF16 = jnp.bfloat16

D_MODEL = 2048
DA_HEADS = 8
DA_HD = 64
DA_WIDTH = DA_HEADS * 2 * DA_HD
RW_HEADS = 16
RW_HD = 64
RW_WIDTH = RW_HEADS * RW_HD
RW_DECAY_LORA = 96
RW_AAA_LORA = 96
RW_GATE_LORA = 256
RW_LORA = RW_DECAY_LORA + RW_AAA_LORA + RW_GATE_LORA
RW_LORA_PAD = 512
RW_PROJ = 3 * RW_WIDTH + RW_LORA
FFN_DIM = 11 * D_MODEL // 4
CHUNK = 64
NORM_EPS = 1e-6
SUBLN_EPS = 1e-5
LNX_EPS = 64e-5
NEG_BIG = -1e30
LOG2_E = math.log2(math.e)

LANES = 128
BF16_ROWS = 16
PAIR = 2 * RW_HD
VMEM_LIMIT = 56 * 1024 * 1024

NT_DIMS = (((1,), (1,)), ((), ()))
TN_DIMS = (((0,), (0,)), ((), ()))


def _dot(a, b):
    return jnp.dot(a, b, preferred_element_type=F32)


def _dot_nt(a, b):
    return lax.dot_general(a, b, NT_DIMS, preferred_element_type=F32)


def _dot_tn(a, b):
    return lax.dot_general(a, b, TN_DIMS, preferred_element_type=F32)


def _bf(x):
    return x.astype(BF16)


def _rms(x, g, eps):
    return x * lax.rsqrt(jnp.mean(x * x, axis=-1, keepdims=True) + eps) * g


def _params(*sem):
    return pltpu.CompilerParams(dimension_semantics=sem, vmem_limit_bytes=VMEM_LIMIT)


def _const_spec(shape):
    nd = len(shape)
    return pl.BlockSpec(shape, lambda *_: (0,) * nd, pipeline_mode=pl.Buffered(1))


def _ffn_kernel(*refs, final):
    if final:
        x_ref, g_ref, wg_ref, wu_ref, wo_ref, fg_ref, o_ref, h_scr, acc_scr = refs
    else:
        x_ref, g_ref, wg_ref, wu_ref, wo_ref, o_ref, h_scr, acc_scr = refs
    f = pl.program_id(1)

    @pl.when(f == 0)
    def _():
        h_scr[...] = _bf(_rms(x_ref[...], g_ref[...], NORM_EPS))
        acc_scr[...] = jnp.zeros_like(acc_scr)

    h = h_scr[...]
    gate = _dot(h, wg_ref[...])
    up = _dot(h, wu_ref[...])
    act = _bf(gate * jax.nn.sigmoid(gate) * up)
    acc_scr[...] += _dot(act, wo_ref[...])

    @pl.when(f == pl.num_programs(1) - 1)
    def _():
        y = x_ref[...] + 0.5 * acc_scr[...]
        if final:
            y = _rms(y, fg_ref[...], NORM_EPS)
        o_ref[...] = y


def _ffn(x, g, wi, wo, final_g=None, *, tm=512, tf=512):
    n = x.shape[0]
    tm = min(tm, n)
    nf = FFN_DIM // tf
    final = final_g is not None
    in_specs = [
        pl.BlockSpec((tm, D_MODEL), lambda i, f: (i, 0)),
        _const_spec((1, D_MODEL)),
        pl.BlockSpec((D_MODEL, tf), lambda i, f: (0, f)),
        pl.BlockSpec((D_MODEL, tf), lambda i, f: (0, f + nf)),
        pl.BlockSpec((tf, D_MODEL), lambda i, f: (f, 0)),
    ]
    args = [x, g, wi, wi, wo]
    if final:
        in_specs.append(_const_spec((1, D_MODEL)))
        args.append(final_g)
    return pl.pallas_call(
        functools.partial(_ffn_kernel, final=final),
        out_shape=jax.ShapeDtypeStruct((n, D_MODEL), F32),
        grid=(n // tm, nf),
        in_specs=in_specs,
        out_specs=pl.BlockSpec((tm, D_MODEL), lambda i, f: (i, 0)),
        scratch_shapes=[pltpu.VMEM((tm, D_MODEL), BF16), pltpu.VMEM((tm, D_MODEL), F32)],
        compiler_params=_params("parallel", "arbitrary"),
        name="ffn_final" if final else "ffn",
    )(*args)


def _qkv_kernel(x_ref, g_ref, wq_ref, wk_ref, wv_ref, wvt_ref, q_ref, k_ref, v_ref, kb_ref, vb_ref, *, transposed):
    h = _bf(_rms(x_ref[...], g_ref[...], NORM_EPS))
    q = _dot_nt(wq_ref[...], h) if transposed else _dot(h, wq_ref[...])
    q_ref[...] = _bf(q * (DA_HD ** -0.5 * LOG2_E))
    k = _dot(h, wk_ref[...])
    k_ref[...] = k
    kb_ref[...] = _bf(k)
    v = _dot(h, wv_ref[...])
    v_ref[...] = v
    vb_ref[...] = _bf(_dot_nt(wvt_ref[...], h)) if transposed else _bf(v)


def _qkv_proj(x, g, wq, wk, wv, wvt, *, transposed, tm):
    n = x.shape[0]
    row = lambda w: pl.BlockSpec((tm, w), lambda i: (i, 0))
    rows = lambda w, dt: jax.ShapeDtypeStruct((n, w), dt)
    wspec = _const_spec(wk.shape)
    if transposed:
        slab = pl.BlockSpec((None, DA_WIDTH, tm), lambda i: (i, 0, 0))
        slabs = jax.ShapeDtypeStruct((n // tm, DA_WIDTH, tm), BF16)
        q_spec, q_shape, vb_spec, vb_shape = slab, slabs, slab, slabs
    else:
        q_spec, q_shape, vb_spec, vb_shape = row(DA_WIDTH), rows(DA_WIDTH, BF16), row(DA_WIDTH), rows(DA_WIDTH, BF16)
    return pl.pallas_call(
        functools.partial(_qkv_kernel, transposed=transposed),
        out_shape=(q_shape, rows(DA_WIDTH, F32), rows(DA_WIDTH, F32), rows(DA_WIDTH, BF16), vb_shape),
        grid=(n // tm,),
        in_specs=[row(D_MODEL), _const_spec((1, D_MODEL)), _const_spec(wq.shape), wspec, wspec,
                  _const_spec(wvt.shape)],
        out_specs=(q_spec, row(DA_WIDTH), row(DA_WIDTH), row(DA_WIDTH), vb_spec),
        compiler_params=_params("parallel"),
        name="qkv_proj_t" if transposed else "qkv_proj",
    )(x, g, wq, wk, wv, wvt)


def _rw_proj_kernel(x_ref, g_ref, wrkv_ref, wl_ref, rkv_ref, lora_ref):
    h = _bf(_rms(x_ref[...], g_ref[...], NORM_EPS))
    rkv_ref[...] = _dot(h, wrkv_ref[...])
    lora_ref[...] = _dot(h, wl_ref[...])


def _rw_proj(x, g, wrkv, wl, *, tm=256):
    n = x.shape[0]
    tm = min(tm, n)
    row = lambda w: pl.BlockSpec((tm, w), lambda i: (i, 0))
    return pl.pallas_call(
        _rw_proj_kernel,
        out_shape=(jax.ShapeDtypeStruct((n, 3 * RW_WIDTH), F32),
                   jax.ShapeDtypeStruct((n, RW_LORA_PAD), F32)),
        grid=(n // tm,),
        in_specs=[row(D_MODEL), _const_spec((1, D_MODEL)),
                  _const_spec((D_MODEL, 3 * RW_WIDTH)), _const_spec((D_MODEL, RW_LORA_PAD))],
        out_specs=(row(3 * RW_WIDTH), row(RW_LORA_PAD)),
        compiler_params=_params("parallel"),
        name="rw_proj",
    )(x, g, wrkv, wl)


def _gate_kernel(x_ref, g_ref, wa_ref, wb_ref, ga_ref, gb_ref):
    h = _bf(_rms(x_ref[...], g_ref[...], NORM_EPS))
    ga_ref[...] = _bf(jax.nn.sigmoid(_dot(h, wa_ref[...])))
    gb_ref[...] = _bf(jax.nn.sigmoid(_dot(h, wb_ref[...])))


def _gate_proj(x, g, wa, wb, *, tm=256):
    n = x.shape[0]
    tm = min(tm, n)
    row = pl.BlockSpec((tm, D_MODEL), lambda i: (i, 0))
    wspec = _const_spec((D_MODEL, D_MODEL))
    return pl.pallas_call(
        _gate_kernel,
        out_shape=(jax.ShapeDtypeStruct((n, D_MODEL), BF16),) * 2,
        grid=(n // tm,),
        in_specs=[row, _const_spec((1, D_MODEL)), wspec, wspec],
        out_specs=(row, row),
        compiler_params=_params("parallel"),
        name="gate_proj",
    )(x, g, wa, wb)


def _stack_maps(q):
    lane = lax.broadcasted_iota(jnp.int32, q.shape, 1)
    zero = jnp.zeros_like(q)
    return jnp.concatenate([jnp.where(lane < DA_HD, q, zero), jnp.where(lane >= DA_HD, q, zero)], axis=0)


def _online_softmax_step(qst, kb, vb, m_scr, l_scr, acc_scr, mask=None):
    s = _dot_nt(qst, kb)
    if mask is not None:
        s = jnp.where(mask, s, NEG_BIG)
    m_prev = m_scr[...]
    m_next = jnp.maximum(m_prev, jnp.max(s, axis=1, keepdims=True))
    alpha = jnp.exp2(m_prev - m_next)
    p = jnp.exp2(s - m_next)
    l_scr[...] = alpha * l_scr[...] + jnp.sum(p, axis=1, keepdims=True)
    acc_scr[...] = alpha * acc_scr[...] + _dot(_bf(p), vb)
    m_scr[...] = m_next


def _attn_init(m_scr, l_scr, acc_scr):
    m_scr[...] = jnp.full_like(m_scr, NEG_BIG)
    l_scr[...] = jnp.zeros_like(l_scr)
    acc_scr[...] = jnp.zeros_like(acc_scr)


def _attn_finalize(tq, lam_refs, sg_ref, o_ref, l_scr, acc_scr, lam_init):
    lq1, lk1, lq2, lk2 = lam_refs
    lam = (jnp.exp(jnp.sum(lq1[...] * lk1[...], axis=-1, keepdims=True))
           - jnp.exp(jnp.sum(lq2[...] * lk2[...], axis=-1, keepdims=True)) + lam_init)
    o = acc_scr[...] / l_scr[...]
    o = o[:tq] - lam * o[tq:]
    o = o * lax.rsqrt(jnp.mean(o * o, axis=-1, keepdims=True) + SUBLN_EPS)
    o_ref[...] = (o * sg_ref[...] * (1.0 - lam_init)).astype(o_ref.dtype)


ATTN_STRIP = 256
ATTN_TQ = 512
SAMPLE_SLOTS = 4


def _attn_prompt_kernel(qt_ref, k_ref, vt_ref, lq1, lk1, lq2, lk2, sg_ref, o_ref,
                        qst_scr, m_scr, acc_scr, s_scr, p_scr, alpha_scr, *, tq, lam_init):
    qi = pl.program_id(2)
    n_strips = 2 * tq // ATTN_STRIP
    last = n_strips - 1
    strip = lambda c: slice(c * ATTN_STRIP, (c + 1) * ATTN_STRIP)

    qt = qt_ref[...]
    feat = lax.broadcasted_iota(jnp.int32, qt.shape, 0)
    zero = jnp.zeros_like(qt)
    qst_scr[:, 0:tq] = jnp.where(feat < DA_HD, qt, zero)
    qst_scr[:, tq:2 * tq] = jnp.where(feat >= DA_HD, qt, zero)
    m_scr[...] = jnp.full_like(m_scr, NEG_BIG)
    acc_scr[...] = jnp.zeros_like(acc_scr)
    p_scr[...] = jnp.zeros_like(p_scr)
    alpha_scr[...] = jnp.ones_like(alpha_scr)

    def key_block(j):
        return k_ref[pl.ds(pl.multiple_of(j * tq, tq), tq), :]

    def value_block(j):
        return jnp.concatenate([vt_ref[j], jnp.ones((BF16_ROWS, tq), BF16)], axis=0)

    def diag_mask(c, s):
        kc = lax.broadcasted_iota(jnp.int32, (tq, ATTN_STRIP), 0) // CHUNK
        qc = ((lax.broadcasted_iota(jnp.int32, (tq, ATTN_STRIP), 1) + c * ATTN_STRIP) % tq) // CHUNK
        return jnp.where(kc <= qc, s, NEG_BIG)

    def scores(kb, c, diag=False):
        s = _dot(kb, qst_scr[:, strip(c)])
        return diag_mask(c, s) if diag else s

    def softmax(c, s):
        cs = strip(c)
        m_prev = m_scr[:, cs]
        m_next = jnp.maximum(m_prev, jnp.max(s, axis=0, keepdims=True))
        m_scr[:, cs] = m_next
        return _bf(jnp.exp2(s - m_next)), jnp.exp2(m_prev - m_next)

    def accumulate(c, vt, p, alpha):
        cs = strip(c)
        acc_scr[:, cs] = alpha * acc_scr[:, cs] + _dot(vt, p)

    def run_block(j, vt_prev, diag):
        kb = key_block(j)
        vt = value_block(j)
        s_next = scores(kb, 1, diag)
        p, alpha = softmax(0, diag_mask(0, s_scr[...]) if diag else s_scr[...])
        accumulate(last, vt_prev, p_scr[...], alpha_scr[...])
        for c in range(1, n_strips):
            s = s_next
            if c < last:
                s_next = scores(kb, c + 1, diag)
            elif not diag:
                s_scr[...] = scores(key_block(j + 1), 0)
            accumulate(c - 1, vt, p, alpha)
            p, alpha = softmax(c, s)
        return vt, p, alpha

    s_scr[...] = scores(key_block(0), 0)

    def body(j, carry):
        vt_prev = value_block(jnp.maximum(j - 1, 0))
        _, p, alpha = run_block(j, vt_prev, False)
        p_scr[...] = p
        alpha_scr[...] = alpha
        return carry

    lax.fori_loop(0, qi, body, 0)
    vt, p, alpha = run_block(qi, value_block(jnp.maximum(qi - 1, 0)), True)
    accumulate(last, vt, p, alpha)

    lam = (jnp.exp(jnp.sum(lq1[...] * lk1[...], axis=-1, keepdims=True))
           - jnp.exp(jnp.sum(lq2[...] * lk2[...], axis=-1, keepdims=True)) + lam_init)
    o = acc_scr[0:LANES, :] / acc_scr[LANES:LANES + 1, :]
    o = o[:, :tq] - lam * o[:, tq:]
    o = o * lax.rsqrt(jnp.mean(o * o, axis=0, keepdims=True) + SUBLN_EPS)
    o = o * sg_ref[...] * (1.0 - lam_init)
    o_ref[...] = o.T.astype(o_ref.dtype)


def _attn_prompt(qt, kb, vt, lam_vecs, subln_g, nb, seq, lam_init, *, tq):
    nq = seq // tq
    vec = _const_spec((1, DA_HD))
    return pl.pallas_call(
        functools.partial(_attn_prompt_kernel, tq=tq, lam_init=lam_init),
        out_shape=jax.ShapeDtypeStruct((nb * seq, DA_WIDTH), BF16),
        grid=(nb, DA_HEADS, nq),
        in_specs=[pl.BlockSpec((None, LANES, tq), lambda b, h, i: (b * nq + i, h, 0)),
                  pl.BlockSpec((seq, LANES), lambda b, h, i: (b, h)),
                  pl.BlockSpec((nq, LANES, tq), lambda b, h, i: (b, h, 0)),
                  vec, vec, vec, vec, _const_spec((LANES, 1))],
        out_specs=pl.BlockSpec((tq, LANES), lambda b, h, i: (b * nq + i, h)),
        scratch_shapes=[pltpu.VMEM((LANES, 2 * tq), BF16), pltpu.VMEM((1, 2 * tq), F32),
                        pltpu.VMEM((LANES + BF16_ROWS, 2 * tq), F32), pltpu.VMEM((tq, ATTN_STRIP), F32),
                        pltpu.VMEM((tq, ATTN_STRIP), BF16), pltpu.VMEM((1, ATTN_STRIP), F32)],
        compiler_params=_params("parallel", "parallel", "arbitrary"),
        name="attn_prompt",
    )(qt, kb, vt, *lam_vecs, subln_g.reshape(LANES, 1))


def _attn_sample_kernel(q_ref, kn_ref, vn_ref, kc_hbm, vc_hbm, lq1, lk1, lq2, lk2, sg_ref, o_ref,
                        kbuf, vbuf, sem, qst_scr, m_scr, l_scr, acc_scr, *, tq, tk, n_blk, row0, lam_init):
    row = row0 + pl.program_id(0)
    heads = [slice(h * LANES, (h + 1) * LANES) for h in range(DA_HEADS)]
    state = lambda h: (m_scr.at[h], l_scr.at[h], acc_scr.at[h])
    n_items = DA_HEADS * n_blk

    def copies(w, slot):
        h = w // n_blk
        start = pl.multiple_of((w % n_blk) * tk, tk)
        return [pltpu.make_async_copy(c.at[row, pl.ds(start, tk), h, :], buf.at[slot], sem.at[i, slot])
                for i, (c, buf) in enumerate(((kc_hbm, kbuf), (vc_hbm, vbuf)))]

    ahead = SAMPLE_SLOTS - 1
    for w in range(min(ahead, n_items)):
        for cp in copies(w, w):
            cp.start()

    for h, cols in enumerate(heads):
        qst_scr[h] = _stack_maps(q_ref[:, cols])
        _attn_init(*state(h))
        _online_softmax_step(qst_scr[h], kn_ref[:, cols], vn_ref[:, cols], *state(h))

    def body(w, carry):
        slot = w % SAMPLE_SLOTS

        @pl.when(w + ahead < n_items)
        def _():
            for cp in copies(w + ahead, (w + ahead) % SAMPLE_SLOTS):
                cp.start()

        for cp in copies(w, slot):
            cp.wait()
        h = w // n_blk
        _online_softmax_step(qst_scr[h], _bf(kbuf[slot]), _bf(vbuf[slot]), *state(h))
        return carry

    lax.fori_loop(0, n_items, body, 0)

    for h, cols in enumerate(heads):
        _attn_finalize(tq, (lq1, lk1, lq2, lk2), sg_ref, o_ref.at[:, cols], l_scr.at[h], acc_scr.at[h], lam_init)


def _attn_sample(q, kb, vb, cache_k, cache_v, layer, lam_vecs, subln_g, nb, seq, lam_init, *, tk=1024):
    past = cache_k.shape[1]
    tk = min(tk, past)
    qspec = pl.BlockSpec((seq, DA_WIDTH), lambda b: (b, 0))
    hbm = pl.BlockSpec(memory_space=pl.ANY)
    vec = _const_spec((1, DA_HD))
    return pl.pallas_call(
        functools.partial(_attn_sample_kernel, tq=seq, tk=tk, n_blk=past // tk, row0=layer * nb, lam_init=lam_init),
        out_shape=jax.ShapeDtypeStruct((nb * seq, DA_WIDTH), BF16),
        grid=(nb,),
        in_specs=[qspec, qspec, qspec, hbm, hbm, vec, vec, vec, vec, _const_spec((1, LANES))],
        out_specs=qspec,
        scratch_shapes=[pltpu.VMEM((SAMPLE_SLOTS, tk, LANES), F32), pltpu.VMEM((SAMPLE_SLOTS, tk, LANES), F32),
                        pltpu.SemaphoreType.DMA((2, SAMPLE_SLOTS)),
                        pltpu.VMEM((DA_HEADS, 2 * seq, LANES), BF16), pltpu.VMEM((DA_HEADS, 2 * seq, 1), F32),
                        pltpu.VMEM((DA_HEADS, 2 * seq, 1), F32), pltpu.VMEM((DA_HEADS, 2 * seq, LANES), F32)],
        compiler_params=_params("arbitrary"),
        name="attn_sample",
    )(q, kb, vb, cache_k, cache_v, *lam_vecs, subln_g)


def _split3(x):
    hi = _bf(x)
    r1 = x - hi.astype(F32)
    mid = _bf(r1)
    lo = _bf(r1 - mid.astype(F32))
    return hi, mid, lo


def _seg_sum(x, ones_bd):
    hi = _bf(x)
    lo = _bf(x - hi.astype(F32))
    return _dot(hi, ones_bd) + _dot(lo, ones_bd)


def _stack_heads(x):
    lane = lax.broadcasted_iota(jnp.int32, x.shape, 1)
    zero = jnp.zeros_like(x)
    return _bf(jnp.concatenate([jnp.where(lane < RW_HD, x, zero), jnp.where(lane >= RW_HD, x, zero)], axis=0))


def _unit_lower_inverse(a_list, eye, blk16, blk32):
    a16 = [jnp.where(blk16, a, 0.0) for a in a_list]
    a32 = [jnp.where(blk32, a, 0.0) - d for a, d in zip(a_list, a16)]
    a64 = [a - d - e for a, d, e in zip(a_list, a16, a32)]
    t = [eye - d for d in a16]
    q = a16
    for _ in range(3):
        qb = [_bf(x) for x in q]
        q = [_dot(x, x) for x in qb]
        t = [x + _dot(_bf(x), _bf(y)) for x, y in zip(t, q)]
    for off in (a32, a64):
        tb = [_bf(x) for x in t]
        inner = [_bf(_dot(_bf(o), x)) for o, x in zip(off, tb)]
        t = [x - _dot(xb, i) for x, xb, i in zip(t, tb, inner)]
    return t


def _rwkv_kernel(rkv_ref, lora_ref, prkv_ref, plora_ref, s0_ref, mu_rkv_ref, mu_lora_ref, w2_ref,
                 w0_ref, a0_ref, kk_ref, ka_ref, rk_ref, lg_ref, lb_ref,
                 ob_ref, sout_ref, s_scr, prev_rkv_scr, prev_lora_scr, *, group):
    c = pl.program_id(1)

    @pl.when(c == 0)
    def _():
        s_scr[...] = s0_ref[...]
        prev_rkv_scr[...] = prkv_ref[...]
        prev_lora_scr[...] = plora_ref[...]

    row = lax.broadcasted_iota(jnp.int32, (CHUNK, 1), 0)
    ti = lax.broadcasted_iota(jnp.int32, (CHUNK, CHUNK), 0)
    si = lax.broadcasted_iota(jnp.int32, (CHUNK, CHUNK), 1)
    tri = jnp.where(ti >= si, 1.0, 0.0).astype(BF16)

    def token_shift(u, prev, mu):
        u_prev = jnp.where(row == 0, prev, pltpu.roll(u, 1, 0))
        return u + mu * (u_prev - u)

    def per_token(s):
        u = rkv_ref[s]
        ul = lora_ref[s]
        us = token_shift(u, prev_rkv_scr[s], mu_rkv_ref[...])
        usl = token_shift(ul, prev_lora_scr[s], mu_lora_ref[...])
        prev_rkv_scr[s] = u[CHUNK - 1:CHUNK, :]
        prev_lora_scr[s] = ul[CHUNK - 1:CHUNK, :]
        lane_l = lax.broadcasted_iota(jnp.int32, usl.shape, 1)
        act = jnp.where(lane_l < RW_DECAY_LORA, jnp.tanh(usl),
                        jnp.where(lane_l < RW_DECAY_LORA + RW_AAA_LORA, usl, jax.nn.sigmoid(usl)))
        lin = _dot(_bf(act), w2_ref[...])
        wx = -(w0_ref[...] + lin[:, 0:RW_WIDTH])
        w = -(jnp.maximum(wx, 0.0) + jnp.log(1.0 + jnp.exp(-jnp.abs(wx)))) - 0.5
        log_decay = -jnp.exp(w)
        cs = sum(_dot(tri, part) for part in _split3(log_decay))
        return dict(r=us[:, 0:RW_WIDTH], k=us[:, RW_WIDTH:2 * RW_WIDTH], v=us[:, 2 * RW_WIDTH:3 * RW_WIDTH],
                    a=jax.nn.sigmoid(a0_ref[...] + lin[:, RW_WIDTH:2 * RW_WIDTH]),
                    g=lin[:, 2 * RW_WIDTH:3 * RW_WIDTH], cs=cs, log_decay=log_decay)

    tok = [per_token(s) for s in range(group)]

    ri = lax.broadcasted_iota(jnp.int32, (PAIR, PAIR), 0)
    ci = lax.broadcasted_iota(jnp.int32, (PAIR, PAIR), 1)
    eye = jnp.where(ri == ci, 1.0, 0.0).astype(F32)
    same_head = (ri // RW_HD) == (ci // RW_HD)
    ones_bd = jnp.where(same_head, 1.0, 0.0).astype(BF16)
    strict = same_head & (ri > ci)
    incl = same_head & (ri >= ci)
    blk16 = (ri // 16) == (ci // 16)
    blk32 = (ri // 32) == (ci // 32)

    units = [(s, p) for s in range(group) for p in range(RW_HEADS // 2)]
    sls = [slice(p * PAIR, (p + 1) * PAIR) for _, p in units]
    col = lambda name: [tok[s][name][:, sl] for (s, _), sl in zip(units, sls)]
    cat = lambda xs, ys: [jnp.concatenate([x, y], axis=0) for x, y in zip(xs, ys)]

    k_raw, a_p, r_p, v_p, cs, log_decay = col("k"), col("a"), col("r"), col("v"), col("cs"), col("log_decay")
    kk = [x * kk_ref[:, sl] for x, sl in zip(k_raw, sls)]
    kk_ss = [_seg_sum(x * x, ones_bd) for x in kk]
    kk = [x * lax.rsqrt(jnp.maximum(ss, 1e-24)) for x, ss in zip(kk, kk_ss)]
    k_p = [x * (1.0 + (ap - 1.0) * ka_ref[:, sl]) for x, ap, sl in zip(k_raw, a_p, sls)]
    g_fwd = [jnp.exp(x) for x in cs]
    g_inv = [jnp.exp(-x) for x in cs]
    g_prev = [jnp.exp(x - ld) for x, ld in zip(cs, log_decay)]

    ka_s = [_stack_heads(x * gp) for x, gp in zip(kk, g_prev)]
    b_s = [_stack_heads(x * ap * gi) for x, ap, gi in zip(kk, a_p, g_inv)]
    k_s = [_stack_heads(x * gi) for x, gi in zip(k_p, g_inv)]
    r_s = [_stack_heads(x * gf) for x, gf in zip(r_p, g_fwd)]
    v_s = [_stack_heads(x) for x in v_p]
    bk_s = cat(b_s, k_s)

    a_all = [_dot_nt(x, y) for x, y in zip(cat(ka_s, r_s), bk_s)]
    a_ab = [jnp.where(strict, x[:PAIR, :PAIR], 0.0) for x in a_all]
    a_ak = [_bf(jnp.where(strict, x[:PAIR, PAIR:], 0.0)) for x in a_all]
    a_rb = [_bf(jnp.where(incl, x[PAIR:, :PAIR], 0.0)) for x in a_all]
    a_rk = [_bf(jnp.where(incl, x[PAIR:, PAIR:], 0.0)) for x in a_all]
    t_inv = [_bf(t) for t in _unit_lower_inverse(a_ab, eye, blk16, blk32)]

    s_old = [s_scr[s, p] for s, p in units]
    s_b = [_bf(x) for x in s_old]
    rhs_s = [_dot_nt(x, sb) for x, sb in zip(ka_s, s_b)]
    rhs_v = [_dot(x, vs) for x, vs in zip(a_ak, v_s)]
    sa_b = [_bf(_dot(t, _bf(-(x + y)))) for t, x, y in zip(t_inv, rhs_s, rhs_v)]
    y_s = [_dot_nt(x, sb) for x, sb in zip(r_s, s_b)]
    y_v = [_dot(x, vs) for x, vs in zip(a_rk, v_s)]
    y_sa = [_dot(x, sa) for x, sa in zip(a_rb, sa_b)]
    upd = [_dot_tn(x, y) for x, y in zip(cat(sa_b, v_s), bk_s)]
    for i, (s, p) in enumerate(units):
        s_scr[s, p] = g_fwd[i][CHUNK - 1:CHUNK, :] * (s_old[i] + upd[i])

    y_bd = [x + y + z for x, y, z in zip(y_s, y_v, y_sa)]
    y = [x[:CHUNK] + x[CHUNK:] for x in y_bd]
    mean = [_seg_sum(x, ones_bd) * (1.0 / RW_HD) for x in y]
    bonus = [_seg_sum(rp * kp * rk_ref[:, sl], ones_bd) * vp for rp, kp, vp, sl in zip(r_p, k_p, v_p, sls)]
    yc = [x - m for x, m in zip(y, mean)]
    var = [_seg_sum(x * x, ones_bd) * (1.0 / RW_HD) for x in yc]
    gate = col("g")
    for i, ((s, _), sl) in enumerate(zip(units, sls)):
        yn = yc[i] * lax.rsqrt(var[i] + LNX_EPS) * lg_ref[:, sl] + lb_ref[:, sl]
        ob_ref[s, :, sl] = ((yn + bonus[i]) * gate[i]).astype(ob_ref.dtype)

    @pl.when(c == pl.num_programs(1) - 1)
    def _():
        sout_ref[...] = s_scr[...]


RW_GROUP = 2


def _rwkv(rkv, lora, prev_rkv, prev_lora, s0_bd, w, nb, seq):
    nc = seq // CHUNK
    npair = RW_HEADS // 2
    group = RW_GROUP if nb % RW_GROUP == 0 else 1
    tok = lambda width: pl.BlockSpec((group, CHUNK, width), lambda b, c: (b, c, 0))
    per_seq = lambda width: pl.BlockSpec((group, 1, width), lambda b, c: (b, 0, 0))
    state = pl.BlockSpec((group, npair, PAIR, PAIR), lambda b, c: (b, 0, 0, 0))
    vec = _const_spec((1, RW_WIDTH))
    o_b, s_new = pl.pallas_call(
        functools.partial(_rwkv_kernel, group=group),
        out_shape=(jax.ShapeDtypeStruct((nb, seq, RW_WIDTH), BF16),
                   jax.ShapeDtypeStruct((nb, npair, PAIR, PAIR), F32)),
        grid=(nb // group, nc),
        in_specs=[tok(3 * RW_WIDTH), tok(RW_LORA_PAD), per_seq(3 * RW_WIDTH), per_seq(RW_LORA_PAD), state,
                  _const_spec((1, 3 * RW_WIDTH)), _const_spec((1, RW_LORA_PAD)),
                  _const_spec((RW_LORA_PAD, 3 * RW_WIDTH)), vec, vec, vec, vec, vec, vec, vec],
        out_specs=(tok(RW_WIDTH), state),
        scratch_shapes=[pltpu.VMEM((group, npair, PAIR, PAIR), F32), pltpu.VMEM((group, 1, 3 * RW_WIDTH), F32),
                        pltpu.VMEM((group, 1, RW_LORA_PAD), F32)],
        compiler_params=_params("parallel", "arbitrary"),
        name="rwkv",
    )(rkv.reshape(nb, seq, -1), lora.reshape(nb, seq, -1), prev_rkv, prev_lora, s0_bd,
      w["mu_rkv"], w["mu_lora"], w["w2cat"], w["w0"], w["a0"], w["k_k"], w["k_a"], w["r_k"], w["lnx_g"], w["lnx_b"])
    return o_b.reshape(nb * seq, RW_WIDTH), s_new


def _merge_kernel(x_ref, oa_ref, ob_ref, ga_ref, gb_ref, wa_ref, wb_ref, wo_ref, o_ref):
    merged = (ga_ref[...].astype(F32) * _dot(oa_ref[...], wa_ref[...])
              + gb_ref[...].astype(F32) * _dot(ob_ref[...], wb_ref[...]))
    o_ref[...] = x_ref[...] + _dot(_bf(merged), wo_ref[...])


def _merge(x, oa, ob, ga, gb, wa, wb, wo, *, tm=256):
    n = x.shape[0]
    tm = min(tm, n)
    row = lambda w: pl.BlockSpec((tm, w), lambda i: (i, 0))
    return pl.pallas_call(
        _merge_kernel,
        out_shape=jax.ShapeDtypeStruct((n, D_MODEL), F32),
        grid=(n // tm,),
        in_specs=[row(D_MODEL), row(DA_WIDTH), row(RW_WIDTH), row(D_MODEL), row(D_MODEL),
                  _const_spec((DA_WIDTH, D_MODEL)), _const_spec((RW_WIDTH, D_MODEL)),
                  _const_spec((D_MODEL, D_MODEL))],
        out_specs=row(D_MODEL),
        compiler_params=_params("parallel"),
        name="merge",
    )(x, oa, ob, ga, gb, wa, wb, wo)


def _prep_weights(l, p):
    row = lambda v: v.reshape(1, -1).astype(F32)
    w_in = p["w_in"][l]
    o_rw = 3 * DA_WIDTH
    o_g = o_rw + RW_PROJ
    w_lora = jnp.pad(w_in[:, o_rw + 3 * RW_WIDTH:o_g], ((0, 0), (0, RW_LORA_PAD - RW_LORA)))
    w2cat = jnp.zeros((RW_LORA_PAD, 3 * RW_WIDTH), F32)
    w2cat = w2cat.at[0:RW_DECAY_LORA, 0:RW_WIDTH].set(p["rw_w2"][l])
    w2cat = w2cat.at[RW_DECAY_LORA:RW_DECAY_LORA + RW_AAA_LORA, RW_WIDTH:2 * RW_WIDTH].set(p["rw_a2"][l])
    w2cat = w2cat.at[RW_DECAY_LORA + RW_AAA_LORA:RW_LORA, 2 * RW_WIDTH:].set(p["rw_g2"][l])
    mu = p["shift_mu"][l]
    return dict(
        ffn1_norm=row(p["ffn1_norm"][l]), ffn1_wi=_bf(p["ffn1_wi"][l]), ffn1_wo=_bf(p["ffn1_wo"][l]),
        ffn2_norm=row(p["ffn2_norm"][l]), ffn2_wi=_bf(p["ffn2_wi"][l]), ffn2_wo=_bf(p["ffn2_wo"][l]),
        mix_norm=row(p["mix_norm"][l]),
        wq=_bf(w_in[:, 0:DA_WIDTH]), wk=_bf(w_in[:, DA_WIDTH:2 * DA_WIDTH]), wv=_bf(w_in[:, 2 * DA_WIDTH:o_rw]),
        wq_t=_bf(w_in[:, 0:DA_WIDTH].T), wv_t=_bf(w_in[:, 2 * DA_WIDTH:o_rw].T),
        w_rkv=_bf(w_in[:, o_rw:o_rw + 3 * RW_WIDTH]), w_lora=_bf(w_lora),
        w_ga=_bf(w_in[:, o_g:o_g + D_MODEL]), w_gb=_bf(w_in[:, o_g + D_MODEL:]),
        lam_vecs=tuple(row(p[n][l]) for n in ("lambda_q1", "lambda_k1", "lambda_q2", "lambda_k2")),
        subln_g=row(p["subln_g"][l]),
        mu_rkv=row(mu[:3 * RW_WIDTH]), mu_lora=row(jnp.pad(mu[3 * RW_WIDTH:], (0, RW_LORA_PAD - RW_LORA))),
        w2cat=_bf(w2cat), w0=row(p["rw_w0"][l]), a0=row(p["rw_a0"][l]), k_k=row(p["rw_k_k"][l]),
        k_a=row(p["rw_k_a"][l]), r_k=row(p["rw_r_k"][l]), lnx_g=row(p["rw_lnx_g"][l]), lnx_b=row(p["rw_lnx_b"][l]),
        w_proj_a=_bf(p["w_proj_a"][l]), w_proj_b=_bf(p["w_proj_b"][l]), w_out=_bf(p["w_out"][l]),
    )


def _state_to_pairs(s):
    nb = s.shape[0]
    s = s.reshape(nb, RW_HEADS // 2, 2, RW_HD, 1, RW_HD)
    sel = jnp.eye(2, dtype=s.dtype).reshape(1, 1, 2, 1, 2, 1)
    return (s * sel).reshape(nb, RW_HEADS // 2, PAIR, PAIR)


def _pairs_to_state(s):
    nb = s.shape[0]
    s = s.reshape(nb, RW_HEADS // 2, 2, RW_HD, 2, RW_HD)
    return jnp.stack([s[:, :, 0, :, 0, :], s[:, :, 1, :, 1, :]], axis=2).reshape(nb, RW_HEADS, RW_HD, RW_HD)


def _layer(x, l, nb, seq, w, cache_k, cache_v, s0, prev, final_g):
    lam_init = 0.8 - 0.6 * math.exp(-0.3 * l)
    x = _ffn(x, w["ffn1_norm"], w["ffn1_wi"], w["ffn1_wo"])
    prompt = cache_k is None
    tq = min(ATTN_TQ, seq)
    q, k, v, kb, vb = _qkv_proj(x, w["mix_norm"], w["wq_t"] if prompt else w["wq"], w["wk"], w["wv"], w["wv_t"],
                                transposed=prompt, tm=tq if prompt else min(256, nb * seq))
    rkv, lora = _rw_proj(x, w["mix_norm"], w["w_rkv"], w["w_lora"])
    ga, gb = _gate_proj(x, w["mix_norm"], w["w_ga"], w["w_gb"])
    if prompt:
        o_a = _attn_prompt(q, kb, vb, w["lam_vecs"], w["subln_g"], nb, seq, lam_init, tq=tq)
        s0 = jnp.zeros((nb, RW_HEADS, RW_HD, RW_HD), F32)
        prev = jnp.zeros((nb, 1, RW_PROJ), F32)
    else:
        rows = lambda c: c.reshape((-1,) + c.shape[2:])
        o_a = _attn_sample(q, kb, vb, rows(cache_k), rows(cache_v), l, w["lam_vecs"], w["subln_g"], nb, seq,
                           lam_init)
    prev_rkv = prev[:, :, :3 * RW_WIDTH]
    prev_lora = jnp.pad(prev[:, :, 3 * RW_WIDTH:], ((0, 0), (0, 0), (0, RW_LORA_PAD - RW_LORA)))
    o_b, s_new = _rwkv(rkv, lora, prev_rkv, prev_lora, _state_to_pairs(s0), w, nb, seq)
    x = _merge(x, o_a, o_b, ga, gb, w["w_proj_a"], w["w_proj_b"], w["w_out"])
    x = _ffn(x, w["ffn2_norm"], w["ffn2_wi"], w["ffn2_wo"], final_g)
    last = jnp.concatenate([rkv.reshape(nb, seq, -1)[:, -1:], lora.reshape(nb, seq, -1)[:, -1:, :RW_LORA]], axis=-1)
    return (x, k.reshape(nb, seq, DA_HEADS, 2 * DA_HD), v.reshape(nb, seq, DA_HEADS, 2 * DA_HD),
            _pairs_to_state(s_new), last)


def _run_stream(x, weights, final_norm, caches):
    nb, seq, _ = x.shape
    x = x.reshape(nb * seq, D_MODEL)
    ks, vs, ss, shs = [], [], [], []
    depth = len(weights)
    for l, w in enumerate(weights):
        final_g = final_norm.reshape(1, -1) if l == depth - 1 else None
        if caches is None:
            ck = cv = s0 = prev = None
        else:
            ck, cv, s0, prev = caches[0], caches[1], caches[2][l], caches[3][l]
        x, k, v, s, sh = _layer(x, l, nb, seq, w, ck, cv, s0, prev, final_g)
        ks.append(k); vs.append(v); ss.append(s); shs.append(sh)
    return x.reshape(nb, seq, D_MODEL), jnp.stack(ks), jnp.stack(vs), jnp.stack(ss), jnp.stack(shs)


def kernel(x_prompt, x_sample, cache_k, cache_v, state_rwkv, state_shift, ffn1_norm, ffn1_wi, ffn1_wo, mix_norm, w_in, lambda_q1, lambda_k1, lambda_q2, lambda_k2, subln_g, shift_mu, rw_w0, rw_w2, rw_a0, rw_a2, rw_g2, rw_k_k, rw_k_a, rw_r_k, rw_lnx_g, rw_lnx_b, w_proj_a, w_proj_b, w_out, ffn2_norm, ffn2_wi, ffn2_wo, final_norm):
    params = dict(ffn1_norm=ffn1_norm, ffn1_wi=ffn1_wi, ffn1_wo=ffn1_wo, mix_norm=mix_norm, w_in=w_in,
                  lambda_q1=lambda_q1, lambda_k1=lambda_k1, lambda_q2=lambda_q2, lambda_k2=lambda_k2,
                  subln_g=subln_g, shift_mu=shift_mu, rw_w0=rw_w0, rw_w2=rw_w2, rw_a0=rw_a0, rw_a2=rw_a2,
                  rw_g2=rw_g2, rw_k_k=rw_k_k, rw_k_a=rw_k_a, rw_r_k=rw_r_k, rw_lnx_g=rw_lnx_g, rw_lnx_b=rw_lnx_b,
                  w_proj_a=w_proj_a, w_proj_b=w_proj_b, w_out=w_out, ffn2_norm=ffn2_norm, ffn2_wi=ffn2_wi,
                  ffn2_wo=ffn2_wo)
    weights = [_prep_weights(l, params) for l in range(ffn1_wi.shape[0])]
    y_p, k_p, v_p, s_p, sh_p = _run_stream(x_prompt, weights, final_norm, None)
    y_s, k_s, v_s, s_s, sh_s = _run_stream(x_sample, weights, final_norm,
                                           (cache_k, cache_v, state_rwkv, state_shift))
    return (y_p, y_s, k_p, v_p, s_p, sh_p, k_s, v_s, s_s, sh_s)
```

```python
import functools
import math

import jax
import jax.numpy as jnp
from jax import lax
from jax.experimental import pallas as pl
from jax.experimental.pallas import tpu as pltpu

F32 = jnp.float32
BF16 = jnp.bfloat16

D_MODEL = 2048
DA_HEADS = 8
DA_HD = 64
DA_WIDTH = DA_HEADS * 2 * DA_HD
RW_HEADS = 16
RW_HD = 64
RW_WIDTH = RW_HEADS * RW_HD
RW_DECAY_LORA = 96
RW_AAA_LORA = 96
RW_GATE_LORA = 256
RW_LORA = RW_DECAY_LORA + RW_AAA_LORA + RW_GATE_LORA
RW_LORA_PAD = 512
RW_PROJ = 3 * RW_WIDTH + RW_LORA
FFN_DIM = 11 * D_MODEL // 4
CHUNK = 64
NORM_EPS = 1e-6
SUBLN_EPS = 1e-5
LNX_EPS = 64e-5
NEG_BIG = -1e30
LOG2_E = math.log2(math.e)

LANES = 128
BF16_ROWS = 16
PAIR = 2 * RW_HD
VMEM_LIMIT = 56 * 1024 * 1024

NT_DIMS = (((1,), (1,)), ((), ()))
TN_DIMS = (((0,), (0,)), ((), ()))


def _dot(a, b):
    return jnp.dot(a, b, preferred_element_type=F32)


def _dot_nt(a, b):
    return lax.dot_general(a, b, NT_DIMS, preferred_element_type=F32)


def _dot_tn(a, b):
    return lax.dot_general(a, b, TN_DIMS, preferred_element_type=F32)


def _bf(x):
    return x.astype(BF16)


def _rms(x, g, eps):
    return x * lax.rsqrt(jnp.mean(x * x, axis=-1, keepdims=True) + eps) * g


def _params(*sem):
    return pltpu.CompilerParams(dimension_semantics=sem, vmem_limit_bytes=VMEM_LIMIT)


def _const_spec(shape):
    nd = len(shape)
    return pl.BlockSpec(shape, lambda *_: (0,) * nd, pipeline_mode=pl.Buffered(1))


def _ffn_kernel(*refs, final):
    if final:
        x_ref, g_ref, wg_ref, wu_ref, wo_ref, fg_ref, o_ref, h_scr, acc_scr = refs
    else:
        x_ref, g_ref, wg_ref, wu_ref, wo_ref, o_ref, h_scr, acc_scr = refs
    f = pl.program_id(1)

    @pl.when(f == 0)
    def _():
        h_scr[...] = _bf(_rms(x_ref[...], g_ref[...], NORM_EPS))
        acc_scr[...] = jnp.zeros_like(acc_scr)

    h = h_scr[...]
    gate = _dot(h, wg_ref[...])
    up = _dot(h, wu_ref[...])
    act = _bf(gate * jax.nn.sigmoid(gate) * up)
    acc_scr[...] += _dot(act, wo_ref[...])

    @pl.when(f == pl.num_programs(1) - 1)
    def _():
        y = x_ref[...] + 0.5 * acc_scr[...]
        if final:
            y = _rms(y, fg_ref[...], NORM_EPS)
        o_ref[...] = y


def _ffn(x, g, wi, wo, final_g=None, *, tm=512, tf=512):
    n = x.shape[0]
    tm = min(tm, n)
    nf = FFN_DIM // tf
    final = final_g is not None
    in_specs = [
        pl.BlockSpec((tm, D_MODEL), lambda i, f: (i, 0)),
        _const_spec((1, D_MODEL)),
        pl.BlockSpec((D_MODEL, tf), lambda i, f: (0, f)),
        pl.BlockSpec((D_MODEL, tf), lambda i, f: (0, f + nf)),
        pl.BlockSpec((tf, D_MODEL), lambda i, f: (f, 0)),
    ]
    args = [x, g, wi, wi, wo]
    if final:
        in_specs.append(_const_spec((1, D_MODEL)))
        args.append(final_g)
    return pl.pallas_call(
        functools.partial(_ffn_kernel, final=final),
        out_shape=jax.ShapeDtypeStruct((n, D_MODEL), F32),
        grid=(n // tm, nf),
        in_specs=in_specs,
        out_specs=pl.BlockSpec((tm, D_MODEL), lambda i, f: (i, 0)),
        scratch_shapes=[pltpu.VMEM((tm, D_MODEL), BF16), pltpu.VMEM((tm, D_MODEL), F32)],
        compiler_params=_params("parallel", "arbitrary"),
        name="ffn_final" if final else "ffn",
    )(*args)


def _qkv_kernel(x_ref, g_ref, wq_ref, wk_ref, wv_ref, wvt_ref, q_ref, k_ref, v_ref, kb_ref, vb_ref, *, transposed):
    h = _bf(_rms(x_ref[...], g_ref[...], NORM_EPS))
    q = _dot_nt(wq_ref[...], h) if transposed else _dot(h, wq_ref[...])
    q_ref[...] = _bf(q * (DA_HD ** -0.5 * LOG2_E))
    k = _dot(h, wk_ref[...])
    k_ref[...] = k
    kb_ref[...] = _bf(k)
    v = _dot(h, wv_ref[...])
    v_ref[...] = v
    vb_ref[...] = _bf(_dot_nt(wvt_ref[...], h)) if transposed else _bf(v)


def _qkv_proj(x, g, wq, wk, wv, wvt, *, transposed, tm):
    n = x.shape[0]
    row = lambda w: pl.BlockSpec((tm, w), lambda i: (i, 0))
    rows = lambda w, dt: jax.ShapeDtypeStruct((n, w), dt)
    wspec = _const_spec(wk.shape)
    if transposed:
        slab = pl.BlockSpec((None, DA_WIDTH, tm), lambda i: (i, 0, 0))
        slabs = jax.ShapeDtypeStruct((n // tm, DA_WIDTH, tm), BF16)
        q_spec, q_shape, vb_spec, vb_shape = slab, slabs, slab, slabs
    else:
        q_spec, q_shape, vb_spec, vb_shape = row(DA_WIDTH), rows(DA_WIDTH, BF16), row(DA_WIDTH), rows(DA_WIDTH, BF16)
    return pl.pallas_call(
        functools.partial(_qkv_kernel, transposed=transposed),
        out_shape=(q_shape, rows(DA_WIDTH, F32), rows(DA_WIDTH, F32), rows(DA_WIDTH, BF16), vb_shape),
        grid=(n // tm,),
        in_specs=[row(D_MODEL), _const_spec((1, D_MODEL)), _const_spec(wq.shape), wspec, wspec,
                  _const_spec(wvt.shape)],
        out_specs=(q_spec, row(DA_WIDTH), row(DA_WIDTH), row(DA_WIDTH), vb_spec),
        compiler_params=_params("parallel"),
        name="qkv_proj_t" if transposed else "qkv_proj",
    )(x, g, wq, wk, wv, wvt)


def _rw_proj_kernel(x_ref, g_ref, wrkv_ref, wl_ref, rkv_ref, lora_ref):
    h = _bf(_rms(x_ref[...], g_ref[...], NORM_EPS))
    rkv_ref[...] = _dot(h, wrkv_ref[...])
    lora_ref[...] = _dot(h, wl_ref[...])


def _rw_proj(x, g, wrkv, wl, *, tm=256):
    n = x.shape[0]
    tm = min(tm, n)
    row = lambda w: pl.BlockSpec((tm, w), lambda i: (i, 0))
    return pl.pallas_call(
        _rw_proj_kernel,
        out_shape=(jax.ShapeDtypeStruct((n, 3 * RW_WIDTH), F32),
                   jax.ShapeDtypeStruct((n, RW_LORA_PAD), F32)),
        grid=(n // tm,),
        in_specs=[row(D_MODEL), _const_spec((1, D_MODEL)),
                  _const_spec((D_MODEL, 3 * RW_WIDTH)), _const_spec((D_MODEL, RW_LORA_PAD))],
        out_specs=(row(3 * RW_WIDTH), row(RW_LORA_PAD)),
        compiler_params=_params("parallel"),
        name="rw_proj",
    )(x, g, wrkv, wl)


def _gate_kernel(x_ref, g_ref, wa_ref, wb_ref, ga_ref, gb_ref):
    h = _bf(_rms(x_ref[...], g_ref[...], NORM_EPS))
    ga_ref[...] = _bf(jax.nn.sigmoid(_dot(h, wa_ref[...])))
    gb_ref[...] = _bf(jax.nn.sigmoid(_dot(h, wb_ref[...])))


def _gate_proj(x, g, wa, wb, *, tm=256):
    n = x.shape[0]
    tm = min(tm, n)
    row = pl.BlockSpec((tm, D_MODEL), lambda i: (i, 0))
    wspec = _const_spec((D_MODEL, D_MODEL))
    return pl.pallas_call(
        _gate_kernel,
        out_shape=(jax.ShapeDtypeStruct((n, D_MODEL), BF16),) * 2,
        grid=(n // tm,),
        in_specs=[row, _const_spec((1, D_MODEL)), wspec, wspec],
        out_specs=(row, row),
        compiler_params=_params("parallel"),
        name="gate_proj",
    )(x, g, wa, wb)


def _stack_maps(q):
    lane = lax.broadcasted_iota(jnp.int32, q.shape, 1)
    zero = jnp.zeros_like(q)
    return jnp.concatenate([jnp.where(lane < DA_HD, q, zero), jnp.where(lane >= DA_HD, q, zero)], axis=0)


def _online_softmax_step(qst, kb, vb, m_scr, l_scr, acc_scr, mask=None):
    s = _dot_nt(qst, kb)
    if mask is not None:
        s = jnp.where(mask, s, NEG_BIG)
    m_prev = m_scr[...]
    m_next = jnp.maximum(m_prev, jnp.max(s, axis=1, keepdims=True))
    alpha = jnp.exp2(m_prev - m_next)
    p = jnp.exp2(s - m_next)
    l_scr[...] = alpha * l_scr[...] + jnp.sum(p, axis=1, keepdims=True)
    acc_scr[...] = alpha * acc_scr[...] + _dot(_bf(p), vb)
    m_scr[...] = m_next


def _attn_init(m_scr, l_scr, acc_scr):
    m_scr[...] = jnp.full_like(m_scr, NEG_BIG)
    l_scr[...] = jnp.zeros_like(l_scr)
    acc_scr[...] = jnp.zeros_like(acc_scr)


def _attn_finalize(tq, lam_refs, sg_ref, o_ref, l_scr, acc_scr, lam_init):
    lq1, lk1, lq2, lk2 = lam_refs
    lam = (jnp.exp(jnp.sum(lq1[...] * lk1[...], axis=-1, keepdims=True))
           - jnp.exp(jnp.sum(lq2[...] * lk2[...], axis=-1, keepdims=True)) + lam_init)
    o = acc_scr[...] / l_scr[...]
    o = o[:tq] - lam * o[tq:]
    o = o * lax.rsqrt(jnp.mean(o * o, axis=-1, keepdims=True) + SUBLN_EPS)
    o_ref[...] = (o * sg_ref[...] * (1.0 - lam_init)).astype(o_ref.dtype)


ATTN_STRIP = 256
ATTN_TQ = 512
SAMPLE_SLOTS = 4


def _attn_prompt_kernel(qt_ref, k_ref, vt_ref, lq1, lk1, lq2, lk2, sg_ref, o_ref,
                        qst_scr, m_scr, acc_scr, s_scr, p_scr, alpha_scr, *, tq, lam_init):
    qi = pl.program_id(2)
    n_strips = 2 * tq // ATTN_STRIP
    last = n_strips - 1
    strip = lambda c: slice(c * ATTN_STRIP, (c + 1) * ATTN_STRIP)
    chunks = tq // CHUNK

    qt = qt_ref[...]
    feat = lax.broadcasted_iota(jnp.int32, qt.shape, 0)
    zero = jnp.zeros_like(qt)
    qst_scr[:, 0:tq] = jnp.where(feat < DA_HD, qt, zero)
    qst_scr[:, tq:2 * tq] = jnp.where(feat >= DA_HD, qt, zero)
    m_scr[...] = jnp.full_like(m_scr, NEG_BIG)
    acc_scr[...] = jnp.zeros_like(acc_scr)
    p_scr[...] = jnp.zeros_like(p_scr)
    alpha_scr[...] = jnp.ones_like(alpha_scr)

    def key_block(j):
        return k_ref[pl.ds(pl.multiple_of(j * tq, tq), tq), :]

    def value_block(j):
        return jnp.concatenate([vt_ref[j], jnp.ones((BF16_ROWS, tq), BF16)], axis=0)

    def chunk_gap(c):
        kc = lax.broadcasted_iota(jnp.int32, (tq, ATTN_STRIP), 0) // CHUNK
        qc = ((lax.broadcasted_iota(jnp.int32, (tq, ATTN_STRIP), 1) + c * ATTN_STRIP) % tq) // CHUNK
        return kc - qc

    def scores(kb, c, blocks_below_diag=None):
        s = _dot(kb, qst_scr[:, strip(c)])
        if blocks_below_diag is not None:
            s = jnp.where(chunk_gap(c) <= blocks_below_diag * chunks, s, NEG_BIG)
        return s

    def softmax(c, s):
        cs = strip(c)
        m_prev = m_scr[:, cs]
        m_next = jnp.maximum(m_prev, jnp.max(s, axis=0, keepdims=True))
        m_scr[:, cs] = m_next
        return _bf(jnp.exp2(s - m_next)), jnp.exp2(m_prev - m_next)

    def accumulate(c, vt, p, alpha):
        cs = strip(c)
        acc_scr[:, cs] = alpha * acc_scr[:, cs] + _dot(vt, p)

    def run_block(j, vt_prev, diag):
        kb = key_block(j)
        vt = value_block(j)
        mask = 0 if diag else None
        s_next = scores(kb, 1, mask)
        p, alpha = softmax(0, s_scr[...])
        accumulate(last, vt_prev, p_scr[...], alpha_scr[...])
        for c in range(1, n_strips):
            s = s_next
            if c < last:
                s_next = scores(kb, c + 1, mask)
            elif not diag:
                s_scr[...] = scores(key_block(j + 1), 0, qi - (j + 1))
            accumulate(c - 1, vt, p, alpha)
            p, alpha = softmax(c, s)
        return vt, p, alpha

    s_scr[...] = scores(key_block(0), 0, qi)

    def body(j, carry):
        vt_prev = value_block(jnp.maximum(j - 1, 0))
        _, p, alpha = run_block(j, vt_prev, False)
        p_scr[...] = p
        alpha_scr[...] = alpha
        return carry

    lax.fori_loop(0, qi, body, 0)
    vt, p, alpha = run_block(qi, value_block(jnp.maximum(qi - 1, 0)), True)
    accumulate(last, vt, p, alpha)

    lam = (jnp.exp(jnp.sum(lq1[...] * lk1[...], axis=-1, keepdims=True))
           - jnp.exp(jnp.sum(lq2[...] * lk2[...], axis=-1, keepdims=True)) + lam_init)
    o = acc_scr[0:LANES, :] / acc_scr[LANES:LANES + 1, :]
    o = o[:, :tq] - lam * o[:, tq:]
    o = o * lax.rsqrt(jnp.mean(o * o, axis=0, keepdims=True) + SUBLN_EPS)
    o = o * sg_ref[...] * (1.0 - lam_init)
    o_ref[...] = o.T.astype(o_ref.dtype)


def _attn_prompt(qt, kb, vt, lam_vecs, subln_g, nb, seq, lam_init, *, tq):
    nq = seq // tq
    vec = _const_spec((1, DA_HD))
    return pl.pallas_call(
        functools.partial(_attn_prompt_kernel, tq=tq, lam_init=lam_init),
        out_shape=jax.ShapeDtypeStruct((nb * seq, DA_WIDTH), BF16),
        grid=(nb, DA_HEADS, nq),
        in_specs=[pl.BlockSpec((None, LANES, tq), lambda b, h, i: (b * nq + i, h, 0)),
                  pl.BlockSpec((seq, LANES), lambda b, h, i: (b, h)),
                  pl.BlockSpec((nq, LANES, tq), lambda b, h, i: (b, h, 0)),
                  vec, vec, vec, vec, _const_spec((LANES, 1))],
        out_specs=pl.BlockSpec((tq, LANES), lambda b, h, i: (b * nq + i, h)),
        scratch_shapes=[pltpu.VMEM((LANES, 2 * tq), BF16), pltpu.VMEM((1, 2 * tq), F32),
                        pltpu.VMEM((LANES + BF16_ROWS, 2 * tq), F32), pltpu.VMEM((tq, ATTN_STRIP), F32),
                        pltpu.VMEM((tq, ATTN_STRIP), BF16), pltpu.VMEM((1, ATTN_STRIP), F32)],
        compiler_params=_params("parallel", "parallel", "arbitrary"),
        name="attn_prompt",
    )(qt, kb, vt, *lam_vecs, subln_g.reshape(LANES, 1))


def _attn_sample_kernel(q_ref, kn_ref, vn_ref, kc_hbm, vc_hbm, lq1, lk1, lq2, lk2, sg_ref, o_ref,
                        kbuf, vbuf, sem, qst_scr, m_scr, l_scr, acc_scr, *, tq, tk, n_blk, row0, lam_init):
    row = row0 + pl.program_id(0)
    heads = [slice(h * LANES, (h + 1) * LANES) for h in range(DA_HEADS)]
    state = lambda h: (m_scr.at[h], l_scr.at[h], acc_scr.at[h])
    n_items = DA_HEADS * n_blk

    def copies(w, slot):
        h = w // n_blk
        start = pl.multiple_of((w % n_blk) * tk, tk)
        return [pltpu.make_async_copy(c.at[row, pl.ds(start, tk), h, :], buf.at[slot], sem.at[i, slot])
                for i, (c, buf) in enumerate(((kc_hbm, kbuf), (vc_hbm, vbuf)))]

    ahead = SAMPLE_SLOTS - 1
    for w in range(min(ahead, n_items)):
        for cp in copies(w, w):
            cp.start()

    for h, cols in enumerate(heads):
        qst_scr[h] = _stack_maps(q_ref[:, cols])
        _attn_init(*state(h))
        _online_softmax_step(qst_scr[h], kn_ref[:, cols], vn_ref[:, cols], *state(h))

    def body(w, carry):
        slot = w % SAMPLE_SLOTS

        @pl.when(w + ahead < n_items)
        def _():
            for cp in copies(w + ahead, (w + ahead) % SAMPLE_SLOTS):
                cp.start()

        for cp in copies(w, slot):
            cp.wait()
        h = w // n_blk
        _online_softmax_step(qst_scr[h], _bf(kbuf[slot]), _bf(vbuf[slot]), *state(h))
        return carry

    lax.fori_loop(0, n_items, body, 0)

    for h, cols in enumerate(heads):
        _attn_finalize(tq, (lq1, lk1, lq2, lk2), sg_ref, o_ref.at[:, cols], l_scr.at[h], acc_scr.at[h], lam_init)


def _attn_sample(q, kb, vb, cache_k, cache_v, layer, lam_vecs, subln_g, nb, seq, lam_init, *, tk=1024):
    past = cache_k.shape[1]
    tk = min(tk, past)
    qspec = pl.BlockSpec((seq, DA_WIDTH), lambda b: (b, 0))
    hbm = pl.BlockSpec(memory_space=pl.ANY)
    vec = _const_spec((1, DA_HD))
    return pl.pallas_call(
        functools.partial(_attn_sample_kernel, tq=seq, tk=tk, n_blk=past // tk, row0=layer * nb, lam_init=lam_init),
        out_shape=jax.ShapeDtypeStruct((nb * seq, DA_WIDTH), BF16),
        grid=(nb,),
        in_specs=[qspec, qspec, qspec, hbm, hbm, vec, vec, vec, vec, _const_spec((1, LANES))],
        out_specs=qspec,
        scratch_shapes=[pltpu.VMEM((SAMPLE_SLOTS, tk, LANES), F32), pltpu.VMEM((SAMPLE_SLOTS, tk, LANES), F32),
                        pltpu.SemaphoreType.DMA((2, SAMPLE_SLOTS)),
                        pltpu.VMEM((DA_HEADS, 2 * seq, LANES), BF16), pltpu.VMEM((DA_HEADS, 2 * seq, 1), F32),
                        pltpu.VMEM((DA_HEADS, 2 * seq, 1), F32), pltpu.VMEM((DA_HEADS, 2 * seq, LANES), F32)],
        compiler_params=_params("arbitrary"),
        name="attn_sample",
    )(q, kb, vb, cache_k, cache_v, *lam_vecs, subln_g)


def _split3(x):
    hi = _bf(x)
    r1 = x - hi.astype(F32)
    mid = _bf(r1)
    lo = _bf(r1 - mid.astype(F32))
    return hi, mid, lo


def _seg_sum(x, ones_bd):
    hi = _bf(x)
    lo = _bf(x - hi.astype(F32))
    return _dot(hi, ones_bd) + _dot(lo, ones_bd)


def _stack_heads(x):
    lane = lax.broadcasted_iota(jnp.int32, x.shape, 1)
    zero = jnp.zeros_like(x)
    return _bf(jnp.concatenate([jnp.where(lane < RW_HD, x, zero), jnp.where(lane >= RW_HD, x, zero)], axis=0))


def _unit_lower_inverse(a_list, eye, blk16, blk32):
    a16 = [jnp.where(blk16, a, 0.0) for a in a_list]
    a32 = [jnp.where(blk32, a, 0.0) - d for a, d in zip(a_list, a16)]
    a64 = [a - d - e for a, d, e in zip(a_list, a16, a32)]
    t = [eye - d for d in a16]
    q = a16
    for _ in range(3):
        qb = [_bf(x) for x in q]
        q = [_dot(x, x) for x in qb]
        t = [x + _dot(_bf(x), _bf(y)) for x, y in zip(t, q)]
    for off in (a32, a64):
        tb = [_bf(x) for x in t]
        inner = [_bf(_dot(_bf(o), x)) for o, x in zip(off, tb)]
        t = [x - _dot(xb, i) for x, xb, i in zip(t, tb, inner)]
    return t


def _rwkv_kernel(rkv_ref, lora_ref, prkv_ref, plora_ref, s0_ref, mu_rkv_ref, mu_lora_ref, w2_ref,
                 w0_ref, a0_ref, kk_ref, ka_ref, rk_ref, lg_ref, lb_ref,
                 ob_ref, sout_ref, s_scr, prev_rkv_scr, prev_lora_scr, *, group):
    c = pl.program_id(1)

    @pl.when(c == 0)
    def _():
        s_scr[...] = s0_ref[...]
        prev_rkv_scr[...] = prkv_ref[...]
        prev_lora_scr[...] = plora_ref[...]

    row = lax.broadcasted_iota(jnp.int32, (CHUNK, 1), 0)
    ti = lax.broadcasted_iota(jnp.int32, (CHUNK, CHUNK), 0)
    si = lax.broadcasted_iota(jnp.int32, (CHUNK, CHUNK), 1)
    tri = jnp.where(ti >= si, 1.0, 0.0).astype(BF16)

    def token_shift(u, prev, mu):
        u_prev = jnp.where(row == 0, prev, pltpu.roll(u, 1, 0))
        return u + mu * (u_prev - u)

    def per_token(s):
        u = rkv_ref[s]
        ul = lora_ref[s]
        us = token_shift(u, prev_rkv_scr[s], mu_rkv_ref[...])
        usl = token_shift(ul, prev_lora_scr[s], mu_lora_ref[...])
        prev_rkv_scr[s] = u[CHUNK - 1:CHUNK, :]
        prev_lora_scr[s] = ul[CHUNK - 1:CHUNK, :]
        lane_l = lax.broadcasted_iota(jnp.int32, usl.shape, 1)
        act = jnp.where(lane_l < RW_DECAY_LORA, jnp.tanh(usl),
                        jnp.where(lane_l < RW_DECAY_LORA + RW_AAA_LORA, usl, jax.nn.sigmoid(usl)))
        lin = _dot(_bf(act), w2_ref[...])
        wx = -(w0_ref[...] + lin[:, 0:RW_WIDTH])
        w = -(jnp.maximum(wx, 0.0) + jnp.log(1.0 + jnp.exp(-jnp.abs(wx)))) - 0.5
        log_decay = -jnp.exp(w)
        cs = sum(_dot(tri, part) for part in _split3(log_decay))
        return dict(r=us[:, 0:RW_WIDTH], k=us[:, RW_WIDTH:2 * RW_WIDTH], v=us[:, 2 * RW_WIDTH:3 * RW_WIDTH],
                    a=jax.nn.sigmoid(a0_ref[...] + lin[:, RW_WIDTH:2 * RW_WIDTH]),
                    g=lin[:, 2 * RW_WIDTH:3 * RW_WIDTH], cs=cs, log_decay=log_decay)

    tok = [per_token(s) for s in range(group)]

    ri = lax.broadcasted_iota(jnp.int32, (PAIR, PAIR), 0)
    ci = lax.broadcasted_iota(jnp.int32, (PAIR, PAIR), 1)
    eye = jnp.where(ri == ci, 1.0, 0.0).astype(F32)
    same_head = (ri // RW_HD) == (ci // RW_HD)
    ones_bd = jnp.where(same_head, 1.0, 0.0).astype(BF16)
    strict = same_head & (ri > ci)
    incl = same_head & (ri >= ci)
    blk16 = (ri // 16) == (ci // 16)
    blk32 = (ri // 32) == (ci // 32)

    units = [(s, p) for s in range(group) for p in range(RW_HEADS // 2)]
    sls = [slice(p * PAIR, (p + 1) * PAIR) for _, p in units]
    col = lambda name: [tok[s][name][:, sl] for (s, _), sl in zip(units, sls)]
    cat = lambda xs, ys: [jnp.concatenate([x, y], axis=0) for x, y in zip(xs, ys)]

    k_raw, a_p, r_p, v_p, cs, log_decay = col("k"), col("a"), col("r"), col("v"), col("cs"), col("log_decay")
    kk = [x * kk_ref[:, sl] for x, sl in zip(k_raw, sls)]
    kk_ss = [_seg_sum(x * x, ones_bd) for x in kk]
    kk = [x * lax.rsqrt(jnp.maximum(ss, 1e-24)) for x, ss in zip(kk, kk_ss)]
    k_p = [x * (1.0 + (ap - 1.0) * ka_ref[:, sl]) for x, ap, sl in zip(k_raw, a_p, sls)]
    g_fwd = [jnp.exp(x) for x in cs]
    g_inv = [jnp.exp(-x) for x in cs]
    g_prev = [jnp.exp(x - ld) for x, ld in zip(cs, log_decay)]

    ka_s = [_stack_heads(x * gp) for x, gp in zip(kk, g_prev)]
    b_s = [_stack_heads(x * ap * gi) for x, ap, gi in zip(kk, a_p, g_inv)]
    k_s = [_stack_heads(x * gi) for x, gi in zip(k_p, g_inv)]
    r_s = [_stack_heads(x * gf) for x, gf in zip(r_p, g_fwd)]
    v_s = [_stack_heads(x) for x in v_p]
    bk_s = cat(b_s, k_s)

    a_all = [_dot_nt(x, y) for x, y in zip(cat(ka_s, r_s), bk_s)]
    a_ab = [jnp.where(strict, x[:PAIR, :PAIR], 0.0) for x in a_all]
    a_ak = [_bf(jnp.where(strict, x[:PAIR, PAIR:], 0.0)) for x in a_all]
    a_rb = [_bf(jnp.where(incl, x[PAIR:, :PAIR], 0.0)) for x in a_all]
    a_rk = [_bf(jnp.where(incl, x[PAIR:, PAIR:], 0.0)) for x in a_all]
    t_inv = [_bf(t) for t in _unit_lower_inverse(a_ab, eye, blk16, blk32)]

    s_old = [s_scr[s, p] for s, p in units]
    s_b = [_bf(x) for x in s_old]
    eye_b = _bf(eye)
    v_t = [_bf(_dot_nt(eye_b, vs)) for vs in v_s]
    rhs_t = [_dot_nt(sb, x) + _dot_nt(vt, y) for sb, x, vt, y in zip(s_b, ka_s, v_t, a_ak)]
    sa_t = [_bf(_dot_nt(_bf(-x), t)) for x, t in zip(rhs_t, t_inv)]
    y_s = [_dot_nt(x, sb) for x, sb in zip(r_s, s_b)]
    y_v = [_dot(x, vs) for x, vs in zip(a_rk, v_s)]
    y_sa = [_dot_nt(x, sa) for x, sa in zip(a_rb, sa_t)]
    upd = [_dot(sa, b) + _dot(vt, k) for sa, b, vt, k in zip(sa_t, b_s, v_t, k_s)]
    for i, (s, p) in enumerate(units):
        s_scr[s, p] = g_fwd[i][CHUNK - 1:CHUNK, :] * (s_old[i] + upd[i])

    y_bd = [x + y + z for x, y, z in zip(y_s, y_v, y_sa)]
    y = [x[:CHUNK] + x[CHUNK:] for x in y_bd]
    mean = [_seg_sum(x, ones_bd) * (1.0 / RW_HD) for x in y]
    bonus = [_seg_sum(rp * kp * rk_ref[:, sl], ones_bd) * vp for rp, kp, vp, sl in zip(r_p, k_p, v_p, sls)]
    yc = [x - m for x, m in zip(y, mean)]
    var = [_seg_sum(x * x, ones_bd) * (1.0 / RW_HD) for x in yc]
    gate = col("g")
    for i, ((s, _), sl) in enumerate(zip(units, sls)):
        yn = yc[i] * lax.rsqrt(var[i] + LNX_EPS) * lg_ref[:, sl] + lb_ref[:, sl]
        ob_ref[s, :, sl] = ((yn + bonus[i]) * gate[i]).astype(ob_ref.dtype)

    @pl.when(c == pl.num_programs(1) - 1)
    def _():
        sout_ref[...] = s_scr[...]


RW_GROUP = 2


def _rwkv(rkv, lora, prev_rkv, prev_lora, s0_bd, w, nb, seq):
    nc = seq // CHUNK
    npair = RW_HEADS // 2
    group = RW_GROUP if nb % RW_GROUP == 0 else 1
    tok = lambda width: pl.BlockSpec((group, CHUNK, width), lambda b, c: (b, c, 0))
    per_seq = lambda width: pl.BlockSpec((group, 1, width), lambda b, c: (b, 0, 0))
    state = pl.BlockSpec((group, npair, PAIR, PAIR), lambda b, c: (b, 0, 0, 0))
    vec = _const_spec((1, RW_WIDTH))
    o_b, s_new = pl.pallas_call(
        functools.partial(_rwkv_kernel, group=group),
        out_shape=(jax.ShapeDtypeStruct((nb, seq, RW_WIDTH), BF16),
                   jax.ShapeDtypeStruct((nb, npair, PAIR, PAIR), F32)),
        grid=(nb // group, nc),
        in_specs=[tok(3 * RW_WIDTH), tok(RW_LORA_PAD), per_seq(3 * RW_WIDTH), per_seq(RW_LORA_PAD), state,
                  _const_spec((1, 3 * RW_WIDTH)), _const_spec((1, RW_LORA_PAD)),
                  _const_spec((RW_LORA_PAD, 3 * RW_WIDTH)), vec, vec, vec, vec, vec, vec, vec],
        out_specs=(tok(RW_WIDTH), state),
        scratch_shapes=[pltpu.VMEM((group, npair, PAIR, PAIR), F32), pltpu.VMEM((group, 1, 3 * RW_WIDTH), F32),
                        pltpu.VMEM((group, 1, RW_LORA_PAD), F32)],
        compiler_params=_params("parallel", "arbitrary"),
        name="rwkv",
    )(rkv.reshape(nb, seq, -1), lora.reshape(nb, seq, -1), prev_rkv, prev_lora, s0_bd,
      w["mu_rkv"], w["mu_lora"], w["w2cat"], w["w0"], w["a0"], w["k_k"], w["k_a"], w["r_k"], w["lnx_g"], w["lnx_b"])
    return o_b.reshape(nb * seq, RW_WIDTH), s_new


def _merge_kernel(x_ref, oa_ref, ob_ref, ga_ref, gb_ref, wa_ref, wb_ref, wo_ref, o_ref):
    merged = (ga_ref[...].astype(F32) * _dot(oa_ref[...], wa_ref[...])
              + gb_ref[...].astype(F32) * _dot(ob_ref[...], wb_ref[...]))
    o_ref[...] = x_ref[...] + _dot(_bf(merged), wo_ref[...])


def _merge(x, oa, ob, ga, gb, wa, wb, wo, *, tm=256):
    n = x.shape[0]
    tm = min(tm, n)
    row = lambda w: pl.BlockSpec((tm, w), lambda i: (i, 0))
    return pl.pallas_call(
        _merge_kernel,
        out_shape=jax.ShapeDtypeStruct((n, D_MODEL), F32),
        grid=(n // tm,),
        in_specs=[row(D_MODEL), row(DA_WIDTH), row(RW_WIDTH), row(D_MODEL), row(D_MODEL),
                  _const_spec((DA_WIDTH, D_MODEL)), _const_spec((RW_WIDTH, D_MODEL)),
                  _const_spec((D_MODEL, D_MODEL))],
        out_specs=row(D_MODEL),
        compiler_params=_params("parallel"),
        name="merge",
    )(x, oa, ob, ga, gb, wa, wb, wo)


def _prep_weights(l, p):
    row = lambda v: v.reshape(1, -1).astype(F32)
    w_in = p["w_in"][l]
    o_rw = 3 * DA_WIDTH
    o_g = o_rw + RW_PROJ
    w_lora = jnp.pad(w_in[:, o_rw + 3 * RW_WIDTH:o_g], ((0, 0), (0, RW_LORA_PAD - RW_LORA)))
    w2cat = jnp.zeros((RW_LORA_PAD, 3 * RW_WIDTH), F32)
    w2cat = w2cat.at[0:RW_DECAY_LORA, 0:RW_WIDTH].set(p["rw_w2"][l])
    w2cat = w2cat.at[RW_DECAY_LORA:RW_DECAY_LORA + RW_AAA_LORA, RW_WIDTH:2 * RW_WIDTH].set(p["rw_a2"][l])
    w2cat = w2cat.at[RW_DECAY_LORA + RW_AAA_LORA:RW_LORA, 2 * RW_WIDTH:].set(p["rw_g2"][l])
    mu = p["shift_mu"][l]
    return dict(
        ffn1_norm=row(p["ffn1_norm"][l]), ffn1_wi=_bf(p["ffn1_wi"][l]), ffn1_wo=_bf(p["ffn1_wo"][l]),
        ffn2_norm=row(p["ffn2_norm"][l]), ffn2_wi=_bf(p["ffn2_wi"][l]), ffn2_wo=_bf(p["ffn2_wo"][l]),
        mix_norm=row(p["mix_norm"][l]),
        wq=_bf(w_in[:, 0:DA_WIDTH]), wk=_bf(w_in[:, DA_WIDTH:2 * DA_WIDTH]), wv=_bf(w_in[:, 2 * DA_WIDTH:o_rw]),
        wq_t=_bf(w_in[:, 0:DA_WIDTH].T), wv_t=_bf(w_in[:, 2 * DA_WIDTH:o_rw].T),
        w_rkv=_bf(w_in[:, o_rw:o_rw + 3 * RW_WIDTH]), w_lora=_bf(w_lora),
        w_ga=_bf(w_in[:, o_g:o_g + D_MODEL]), w_gb=_bf(w_in[:, o_g + D_MODEL:]),
        lam_vecs=tuple(row(p[n][l]) for n in ("lambda_q1", "lambda_k1", "lambda_q2", "lambda_k2")),
        subln_g=row(p["subln_g"][l]),
        mu_rkv=row(mu[:3 * RW_WIDTH]), mu_lora=row(jnp.pad(mu[3 * RW_WIDTH:], (0, RW_LORA_PAD - RW_LORA))),
        w2cat=_bf(w2cat), w0=row(p["rw_w0"][l]), a0=row(p["rw_a0"][l]), k_k=row(p["rw_k_k"][l]),
        k_a=row(p["rw_k_a"][l]), r_k=row(p["rw_r_k"][l]), lnx_g=row(p["rw_lnx_g"][l]), lnx_b=row(p["rw_lnx_b"][l]),
        w_proj_a=_bf(p["w_proj_a"][l]), w_proj_b=_bf(p["w_proj_b"][l]), w_out=_bf(p["w_out"][l]),
    )


def _state_to_pairs(s):
    nb = s.shape[0]
    s = s.reshape(nb, RW_HEADS // 2, 2, RW_HD, 1, RW_HD)
    sel = jnp.eye(2, dtype=s.dtype).reshape(1, 1, 2, 1, 2, 1)
    return (s * sel).reshape(nb, RW_HEADS // 2, PAIR, PAIR)


def _pairs_to_state(s):
    nb = s.shape[0]
    s = s.reshape(nb, RW_HEADS // 2, 2, RW_HD, 2, RW_HD)
    return jnp.stack([s[:, :, 0, :, 0, :], s[:, :, 1, :, 1, :]], axis=2).reshape(nb, RW_HEADS, RW_HD, RW_HD)


def _layer(x, l, nb, seq, w, cache_k, cache_v, s0, prev, final_g):
    lam_init = 0.8 - 0.6 * math.exp(-0.3 * l)
    x = _ffn(x, w["ffn1_norm"], w["ffn1_wi"], w["ffn1_wo"])
    prompt = cache_k is None
    tq = min(ATTN_TQ, seq)
    q, k, v, kb, vb = _qkv_proj(x, w["mix_norm"], w["wq_t"] if prompt else w["wq"], w["wk"], w["wv"], w["wv_t"],
                                transposed=prompt, tm=tq if prompt else min(256, nb * seq))
    rkv, lora = _rw_proj(x, w["mix_norm"], w["w_rkv"], w["w_lora"])
    ga, gb = _gate_proj(x, w["mix_norm"], w["w_ga"], w["w_gb"])
    if prompt:
        o_a = _attn_prompt(q, kb, vb, w["lam_vecs"], w["subln_g"], nb, seq, lam_init, tq=tq)
        s0 = jnp.zeros((nb, RW_HEADS, RW_HD, RW_HD), F32)
        prev = jnp.zeros((nb, 1, RW_PROJ), F32)
    else:
        rows = lambda c: c.reshape((-1,) + c.shape[2:])
        o_a = _attn_sample(q, kb, vb, rows(cache_k), rows(cache_v), l, w["lam_vecs"], w["subln_g"], nb, seq,
                           lam_init)
    prev_rkv = prev[:, :, :3 * RW_WIDTH]
    prev_lora = jnp.pad(prev[:, :, 3 * RW_WIDTH:], ((0, 0), (0, 0), (0, RW_LORA_PAD - RW_LORA)))
    o_b, s_new = _rwkv(rkv, lora, prev_rkv, prev_lora, _state_to_pairs(s0), w, nb, seq)
    x = _merge(x, o_a, o_b, ga, gb, w["w_proj_a"], w["w_proj_b"], w["w_out"])
    x = _ffn(x, w["ffn2_norm"], w["ffn2_wi"], w["ffn2_wo"], final_g)
    last = jnp.concatenate([rkv.reshape(nb, seq, -1)[:, -1:], lora.reshape(nb, seq, -1)[:, -1:, :RW_LORA]], axis=-1)
    return (x, k.reshape(nb, seq, DA_HEADS, 2 * DA_HD), v.reshape(nb, seq, DA_HEADS, 2 * DA_HD),
            _pairs_to_state(s_new), last)


def _run_stream(x, weights, final_norm, caches):
    nb, seq, _ = x.shape
    x = x.reshape(nb * seq, D_MODEL)
    ks, vs, ss, shs = [], [], [], []
    depth = len(weights)
    for l, w in enumerate(weights):
        final_g = final_norm.reshape(1, -1) if l == depth - 1 else None
        if caches is None:
            ck = cv = s0 = prev = None
        else:
            ck, cv, s0, prev = caches[0], caches[1], caches[2][l], caches[3][l]
        x, k, v, s, sh = _layer(x, l, nb, seq, w, ck, cv, s0, prev, final_g)
        ks.append(k); vs.append(v); ss.append(s); shs.append(sh)
    return x.reshape(nb, seq, D_MODEL), jnp.stack(ks), jnp.stack(vs), jnp.stack(ss), jnp.stack(shs)


def kernel(x_prompt, x_sample, cache_k, cache_v, state_rwkv, state_shift, ffn1_norm, ffn1_wi, ffn1_wo, mix_norm, w_in, lambda_q1, lambda_k1, lambda_q2, lambda_k2, subln_g, shift_mu, rw_w0, rw_w2, rw_a0, rw_a2, rw_g2, rw_k_k, rw_k_a, rw_r_k, rw_lnx_g, rw_lnx_b, w_proj_a, w_proj_b, w_out, ffn2_norm, ffn2_wi, ffn2_wo, final_norm):
    params = dict(ffn1_norm=ffn1_norm, ffn1_wi=ffn1_wi, ffn1_wo=ffn1_wo, mix_norm=mix_norm, w_in=w_in,
                  lambda_q1=lambda_q1, lambda_k1=lambda_k1, lambda_q2=lambda_q2, lambda_k2=lambda_k2,
                  subln_g=subln_g, shift_mu=shift_mu, rw_w0=rw_w0, rw_w2=rw_w2, rw_a0=rw_a0, rw_a2=rw_a2,
                  rw_g2=rw_g2, rw_k_k=rw_k_k, rw_k_a=rw_k_a, rw_r_k=rw_r_k, rw_lnx_g=rw_lnx_g, rw_lnx_b=rw_lnx_b,
                  w_proj_a=w_proj_a, w_proj_b=w_proj_b, w_out=w_out, ffn2_norm=ffn2_norm, ffn2_wi=ffn2_wi,
                  ffn2_wo=ffn2_wo)
    weights = [_prep_weights(l, params) for l in range(ffn1_wi.shape[0])]
    y_p, k_p, v_p, s_p, sh_p = _run_stream(x_prompt, weights, final_norm, None)
    y_s, k_s, v_s, s_s, sh_s = _run_stream(x_sample, weights, final_norm,
                                           (cache_k, cache_v, state_rwkv, state_shift))
    return (y_p, y_s, k_p, v_p, s_p, sh_p, k_s, v_s, s_s, sh_s)
```
